```python
import jax, jax.numpy as jnp
from jax import lax
import numpy as np

D_MODEL = 1024
BATCH = 4
SEQ = 4096
DEPTH = 4
DEC_BATCH = 128
DEC_SEQ = 8
PAST_LEN = 2048
PAGE_SIZE = 128

A_WIDTH = D_MODEL // 2
B_WIDTH = D_MODEL - A_WIDTH
HEAD_DIM = 64
FOX_HEADS = A_WIDTH // HEAD_DIM
Q_BLOCK = 128
FORGET_BIAS = 3.0
RG_BLOCKS = 8
RG_BD = B_WIDTH // RG_BLOCKS
RG_CONV = 4
RG_C = 8.0
POOL_WINDOWS = (2, 4, 8, 16)
POOL_GROUPS = len(POOL_WINDOWS)
POOL_GD = D_MODEL // POOL_GROUPS
POOL_BUF = max(POOL_WINDOWS) - 1
D_FF = 3 * D_MODEL
FFN_CONV = 3
N_AB_LAYERS = (DEPTH + 1) // 2
N_POOL_LAYERS = DEPTH // 2
IN_SPLITS = (A_WIDTH, 2 * A_WIDTH, 3 * A_WIDTH, 3 * A_WIDTH + FOX_HEADS, 3 * A_WIDTH + FOX_HEADS + B_WIDTH)
IN_WIDTH = 3 * A_WIDTH + FOX_HEADS + 2 * B_WIDTH
EPS = 1e-6
NEG = -1e30

kernel_name = "fox_rglru_pool_convffn_step"


def _rmsnorm(x, g):
    xf = x.astype(jnp.float32)
    y = xf * lax.rsqrt(jnp.mean(xf * xf, axis=-1, keepdims=True) + EPS)
    return (y * g.astype(jnp.float32)).astype(x.dtype)


def _causal_dwconv(u, prev, w, b):
    width, t = w.shape[0], u.shape[1]
    ue = jnp.concatenate([prev.astype(u.dtype), u], axis=1)
    y = b
    for j in range(width):
        y = y + ue[:, j:j + t] * w[j]
    return y, ue[:, t:]


def _gather_pages(pool, page_table):
    pages = pool[page_table]
    return pages.reshape((pages.shape[0], pages.shape[1] * pages.shape[2]) + pages.shape[3:])


def _fox_attention(q, k, v, c_q, c_k, q_pos, k_pos):
    bsz, tq, nh, dh = q.shape
    qb = min(Q_BLOCK, tq)
    nb = -(-tq // qb)
    pad = nb * qb - tq
    if pad:
        q = jnp.pad(q, ((0, 0), (0, pad), (0, 0), (0, 0)))
        c_q = jnp.pad(c_q, ((0, 0), (0, pad), (0, 0)), mode="edge")
        q_pos = jnp.pad(q_pos, (0, pad), mode="edge")
    q_blk = q.reshape(bsz, nb, qb, nh, dh).swapaxes(0, 1)
    c_blk = c_q.reshape(bsz, nb, qb, nh).swapaxes(0, 1)
    p_blk = q_pos.reshape(nb, qb)
    c_k_h = jnp.swapaxes(c_k, 1, 2)
    scale = HEAD_DIM ** -0.5

    def one_block(args):
        qi, ci, pi = args
        s = jnp.einsum("bqhd,bkhd->bhqk", qi, k).astype(jnp.float32) * scale
        s = s + jnp.swapaxes(ci, 1, 2)[..., None] - c_k_h[:, :, None, :]
        mask = k_pos[None, :] <= pi[:, None]
        s = jnp.where(mask[None, None], s, NEG)
        prob = jax.nn.softmax(s, axis=-1)
        return jnp.einsum("bhqk,bkhd->bqhd", prob.astype(v.dtype), v)

    out = lax.map(one_block, (q_blk, c_blk, p_blk))
    return out.swapaxes(0, 1).reshape(bsz, nb * qb, nh, dh)[:, :tq]


def _rglru(xc, h0, w_a, b_a, w_x, b_x, lam):
    bsz, t, c = xc.shape
    xb = xc.reshape(bsz, t, RG_BLOCKS, RG_BD)
    r = jax.nn.sigmoid((jnp.einsum("bthi,hij->bthj", xb, w_a).reshape(bsz, t, c) + b_a).astype(jnp.float32))
    i = jax.nn.sigmoid((jnp.einsum("bthi,hij->bthj", xb, w_x).reshape(bsz, t, c) + b_x).astype(jnp.float32))
    log_a = -RG_C * r * jax.nn.softplus(-lam.astype(jnp.float32))
    a = jnp.exp(log_a)
    inp = jnp.sqrt(-jnp.expm1(2.0 * log_a)) * i * xc.astype(jnp.float32)

    def combine(lhs, rhs):
        return (lhs[0] * rhs[0], rhs[0] * lhs[1] + rhs[1])

    a_cum, h_zero = lax.associative_scan(combine, (a, inp), axis=1)
    h = h_zero + a_cum * h0.astype(jnp.float32)[:, None, :]
    return h.astype(xc.dtype), h[:, -1].astype(xc.dtype)


def _mixer_ab(xn, past, h0, conv_prev, p, li):
    bsz, t, _ = xn.shape
    proj = jnp.einsum("btd,de->bte", xn, p["ab_w_in"][li])
    q, k, v, f, xr, gate = jnp.split(proj, IN_SPLITS, axis=-1)
    q = _rmsnorm(q.reshape(bsz, t, FOX_HEADS, HEAD_DIM), p["ab_q_gain"][li])
    k = _rmsnorm(k.reshape(bsz, t, FOX_HEADS, HEAD_DIM), p["ab_k_gain"][li])
    v = v.reshape(bsz, t, FOX_HEADS, HEAD_DIM)
    logf = jax.nn.log_sigmoid((f + p["ab_b_f"][li]).astype(jnp.float32))
    if past is None:
        past_len = 0
        k_all, v_all, lf_all = k, v, logf
    else:
        k_past, v_past, lf_past = past
        past_len = k_past.shape[1]
        k_all = jnp.concatenate([k_past.astype(k.dtype), k], axis=1)
        v_all = jnp.concatenate([v_past.astype(v.dtype), v], axis=1)
        lf_all = jnp.concatenate([lf_past.astype(jnp.float32), logf], axis=1)
    c_all = jnp.cumsum(lf_all, axis=1)
    k_pos = jnp.arange(past_len + t, dtype=jnp.int32)
    attn = _fox_attention(q, k_all, v_all, c_all[:, past_len:], c_all, k_pos[past_len:], k_pos)
    xc, conv_new = _causal_dwconv(xr, conv_prev, p["ab_conv_w"][li], p["ab_conv_b"][li])
    h, h_last = _rglru(xc, h0, p["ab_w_a"][li], p["ab_b_a"][li], p["ab_w_x"][li], p["ab_b_x"][li], p["ab_lambda"][li])
    y_rg = h * jax.nn.gelu(gate)
    merged = jnp.concatenate([attn.reshape(bsz, t, A_WIDTH), y_rg], axis=-1)
    out = jnp.einsum("bte,ed->btd", merged, p["ab_w_out"][li])
    return out, (k, v, logf.astype(xn.dtype), h_last, conv_new)


def _pool_mixer(xn, pos0, prev, w_grp, scale):
    bsz, t, d = xn.shape
    xe = jnp.concatenate([prev.astype(xn.dtype), xn], axis=1)
    cs = jnp.concatenate([jnp.zeros((bsz, 1, d), jnp.float32), jnp.cumsum(xe.astype(jnp.float32), axis=1)], axis=1)
    pos = pos0 + jnp.arange(t, dtype=jnp.int32)
    outs = []
    for g, w in enumerate(POOL_WINDOWS):
        lo, hi = g * POOL_GD, (g + 1) * POOL_GD
        end = cs[:, POOL_BUF + 1:POOL_BUF + 1 + t, lo:hi]
        start = cs[:, POOL_BUF + 1 - w:POOL_BUF + 1 - w + t, lo:hi]
        cnt = jnp.minimum(pos + 1, w).astype(jnp.float32)[None, :, None]
        diff = (end - start) / cnt - xn[:, :, lo:hi].astype(jnp.float32)
        outs.append(jnp.einsum("btc,ce->bte", diff.astype(xn.dtype), w_grp[g]))
    y = jnp.concatenate(outs, axis=-1) * scale
    return y.astype(xn.dtype), xe[:, t:]


def _conv_ffn(xn, prev, w_up, cw, cb, w_down):
    u = jnp.einsum("btd,df->btf", xn, w_up)
    uc, new_prev = _causal_dwconv(u, prev, cw, cb)
    g, val = jnp.split(uc, 2, axis=-1)
    return jnp.einsum("btf,fd->btd", jax.nn.gelu(g) * val, w_down), new_prev


def _trunk(x, pos0, p, paged, rg_h, rg_conv, pool_buf, ffn_buf):
    ks, vs, lfs, hs, cs, pbs, fbs = [], [], [], [], [], [], []
    for layer in range(DEPTH):
        li = layer // 2
        xn = _rmsnorm(x, p["norm_mix"][layer])
        if layer % 2 == 0:
            past = None
            if paged is not None:
                ck, cv, cf, pt = paged
                past = (_gather_pages(ck[li], pt), _gather_pages(cv[li], pt), _gather_pages(cf[li], pt))
            y, (k, v, lf, h_last, conv_new) = _mixer_ab(xn, past, rg_h[li], rg_conv[li], p, li)
            ks.append(k); vs.append(v); lfs.append(lf); hs.append(h_last); cs.append(conv_new)
        else:
            y, pb = _pool_mixer(xn, pos0, pool_buf[li], p["pool_w"][li], p["pool_scale"][li])
            pbs.append(pb)
        x = x + y
        xn = _rmsnorm(x, p["norm_ffn"][layer])
        y, fb = _conv_ffn(xn, ffn_buf[layer], p["ffn_w_up"][layer], p["ffn_conv_w"][layer],
                          p["ffn_conv_b"][layer], p["ffn_w_down"][layer])
        fbs.append(fb)
        x = x + y
    return x, (jnp.stack(ks), jnp.stack(vs), jnp.stack(lfs), jnp.stack(hs), jnp.stack(cs),
               jnp.stack(pbs), jnp.stack(fbs))


def setup_inputs(seed: int = 0) -> dict:
    key = jax.random.key(seed)
    keys = iter(jax.random.split(key, 40))
    f32 = jnp.float32

    def nrm(shape, s=1.0):
        return jax.random.normal(next(keys), shape, f32) * s

    n_pages = PAST_LEN // PAGE_SIZE
    n_used = DEC_BATCH * n_pages
    n_pool_pages = n_used + max(1, n_used // 4)
    perm = jax.random.permutation(next(keys), n_pool_pages)
    page_table = perm[:n_used].reshape(DEC_BATCH, n_pages).astype(jnp.int32)

    x_prompt = nrm((BATCH, SEQ, D_MODEL))
    x_sample = nrm((DEC_BATCH, DEC_SEQ, D_MODEL))
    cache_k = nrm((N_AB_LAYERS, n_pool_pages, PAGE_SIZE, FOX_HEADS, HEAD_DIM))
    cache_v = nrm((N_AB_LAYERS, n_pool_pages, PAGE_SIZE, FOX_HEADS, HEAD_DIM))
    cache_logf = jax.nn.log_sigmoid(FORGET_BIAS + nrm((N_AB_LAYERS, n_pool_pages, PAGE_SIZE, FOX_HEADS)))
    state_rg_h = nrm((N_AB_LAYERS, DEC_BATCH, B_WIDTH), 0.5)
    state_rg_conv = nrm((N_AB_LAYERS, DEC_BATCH, RG_CONV - 1, B_WIDTH))
    state_pool = nrm((N_POOL_LAYERS, DEC_BATCH, POOL_BUF, D_MODEL))
    state_ffn_conv = nrm((DEPTH, DEC_BATCH, FFN_CONV - 1, 2 * D_FF))

    a0 = jax.random.uniform(next(keys), (N_AB_LAYERS, B_WIDTH), f32, minval=0.9, maxval=0.999)
    return {
        "x_prompt": x_prompt,
        "x_sample": x_sample,
        "cache_k": cache_k,
        "cache_v": cache_v,
        "cache_logf": cache_logf,
        "state_rg_h": state_rg_h,
        "state_rg_conv": state_rg_conv,
        "state_pool": state_pool,
        "state_ffn_conv": state_ffn_conv,
        "page_table": page_table,
        "norm_mix": 1.0 + nrm((DEPTH, D_MODEL), 0.02),
        "norm_ffn": 1.0 + nrm((DEPTH, D_MODEL), 0.02),
        "ab_w_in": nrm((N_AB_LAYERS, D_MODEL, IN_WIDTH), D_MODEL ** -0.5),
        "ab_b_f": FORGET_BIAS + nrm((N_AB_LAYERS, FOX_HEADS), 0.1),
        "ab_q_gain": 1.0 + nrm((N_AB_LAYERS, HEAD_DIM), 0.02),
        "ab_k_gain": 1.0 + nrm((N_AB_LAYERS, HEAD_DIM), 0.02),
        "ab_conv_w": nrm((N_AB_LAYERS, RG_CONV, B_WIDTH), RG_CONV ** -0.5),
        "ab_conv_b": nrm((N_AB_LAYERS, B_WIDTH), 0.02),
        "ab_w_a": nrm((N_AB_LAYERS, RG_BLOCKS, RG_BD, RG_BD), RG_BD ** -0.5),
        "ab_b_a": nrm((N_AB_LAYERS, B_WIDTH), 0.02),
        "ab_w_x": nrm((N_AB_LAYERS, RG_BLOCKS, RG_BD, RG_BD), RG_BD ** -0.5),
        "ab_b_x": nrm((N_AB_LAYERS, B_WIDTH), 0.02),
        "ab_lambda": jnp.log(a0) - jnp.log1p(-a0),
        "ab_w_out": nrm((N_AB_LAYERS, D_MODEL, D_MODEL), D_MODEL ** -0.5),
        "pool_w": nrm((N_POOL_LAYERS, POOL_GROUPS, POOL_GD, POOL_GD), POOL_GD ** -0.5),
        "pool_scale": 1.0 + nrm((N_POOL_LAYERS, D_MODEL), 0.1),
        "ffn_w_up": nrm((DEPTH, D_MODEL, 2 * D_FF), D_MODEL ** -0.5),
        "ffn_conv_w": nrm((DEPTH, FFN_CONV, 2 * D_FF), FFN_CONV ** -0.5),
        "ffn_conv_b": nrm((DEPTH, 2 * D_FF), 0.02),
        "ffn_w_down": nrm((DEPTH, D_FF, D_MODEL), D_FF ** -0.5),
    }


def reference(x_prompt, x_sample, cache_k, cache_v, cache_logf, state_rg_h, state_rg_conv, state_pool,
              state_ffn_conv, page_table, norm_mix, norm_ffn, ab_w_in, ab_b_f, ab_q_gain, ab_k_gain,
              ab_conv_w, ab_conv_b, ab_w_a, ab_b_a, ab_w_x, ab_b_x, ab_lambda, ab_w_out, pool_w, pool_scale,
              ffn_w_up, ffn_conv_w, ffn_conv_b, ffn_w_down):
    p = {
        "norm_mix": norm_mix, "norm_ffn": norm_ffn,
        "ab_w_in": ab_w_in, "ab_b_f": ab_b_f, "ab_q_gain": ab_q_gain, "ab_k_gain": ab_k_gain,
        "ab_conv_w": ab_conv_w, "ab_conv_b": ab_conv_b, "ab_w_a": ab_w_a, "ab_b_a": ab_b_a,
        "ab_w_x": ab_w_x, "ab_b_x": ab_b_x, "ab_lambda": ab_lambda, "ab_w_out": ab_w_out,
        "pool_w": pool_w, "pool_scale": pool_scale,
        "ffn_w_up": ffn_w_up, "ffn_conv_w": ffn_conv_w, "ffn_conv_b": ffn_conv_b, "ffn_w_down": ffn_w_down,
    }
    bsz, dt = x_prompt.shape[0], x_prompt.dtype
    z_h = jnp.zeros((N_AB_LAYERS, bsz, B_WIDTH), dt)
    z_c = jnp.zeros((N_AB_LAYERS, bsz, RG_CONV - 1, B_WIDTH), dt)
    z_pool = jnp.zeros((N_POOL_LAYERS, bsz, POOL_BUF, D_MODEL), dt)
    z_ffn = jnp.zeros((DEPTH, bsz, FFN_CONV - 1, 2 * D_FF), dt)
    y_prompt, (k_p, v_p, logf_p, rg_h_p, rg_conv_p, pool_p, ffn_conv_p) = _trunk(
        x_prompt, 0, p, None, z_h, z_c, z_pool, z_ffn)
    past_len = page_table.shape[1] * cache_k.shape[2]
    y_sample, (k_s, v_s, logf_s, rg_h_s, rg_conv_s, pool_s, ffn_conv_s) = _trunk(
        x_sample, past_len, p, (cache_k, cache_v, cache_logf, page_table),
        state_rg_h, state_rg_conv, state_pool, state_ffn_conv)
    return (y_prompt, y_sample, k_p, v_p, logf_p, rg_h_p, rg_conv_p, pool_p, ffn_conv_p,
            k_s, v_s, logf_s, rg_h_s, rg_conv_s, pool_s, ffn_conv_s)
```

```python
import functools

import jax
import jax.numpy as jnp
from jax import lax
from jax.experimental import pallas as pl
from jax.experimental.pallas import tpu as pltpu

D_MODEL = 1024
A_WIDTH = 512
B_WIDTH = 512
HEAD_DIM = 64
FOX_HEADS = 8
RG_CONV = 4
RG_C = 8.0
POOL_WINDOWS = (2, 4, 8, 16)
POOL_GD = 256
POOL_BUF = 15
D_FF = 3072
FFN_CONV = 3
EPS = 1e-6
NEG = -1e30
F_PAD = 128
SUBLANES = 8
F32 = jnp.float32
BF16 = jnp.bfloat16
VMEM_LIMIT_BYTES = 56 * 1024 * 1024


def _round_up(x, m):
    return -(-x // m) * m


def _full_spec(shape):
    return pl.BlockSpec(shape, lambda *_: (0,) * len(shape))


def _params(n_axes):
    return pltpu.CompilerParams(dimension_semantics=("arbitrary",) * n_axes,
                                vmem_limit_bytes=VMEM_LIMIT_BYTES)


def _dot(a, b):
    return jnp.dot(a, b, preferred_element_type=F32)


def _dot_nt(a, b):
    return lax.dot_general(a, b, (((1,), (1,)), ((), ())), preferred_element_type=F32)


def _rms(x, gain):
    y = x * lax.rsqrt(jnp.mean(x * x, axis=-1, keepdims=True) + EPS)
    return y * gain


def _gelu(x):
    return 0.5 * x * (1.0 + jnp.tanh(0.7978845608028654 * (x + 0.044715 * (x * x * x))))


def _softplus(x):
    return jnp.maximum(x, 0.0) + jnp.log1p(jnp.exp(-jnp.abs(x)))


class _Geom:
    def __init__(self, n_seq, rows_per_seq, dil, tile):
        self.n_seq, self.rows_per_seq, self.dil = n_seq, rows_per_seq, dil
        self.tm = min(tile, rows_per_seq)
        assert rows_per_seq % self.tm == 0 and self.tm % dil == 0 and self.tm % SUBLANES == 0
        self.tps = rows_per_seq // self.tm
        self.rows = n_seq * rows_per_seq
        self.n_tiles = self.rows // self.tm

    def halo(self, width):
        return _round_up((width - 1) * self.dil, SUBLANES)


def _conv_from_ext(ext_ref, w_ref, bias, width, dil, tm, hp):
    y = bias
    for j in range(width):
        off = hp - (width - 1 - j) * dil
        y = y + ext_ref[off:off + tm, :] * w_ref[j:j + 1, :]
    return y


def _ab_in_kernel(x_ref, g_ref, wqk_ref, wv_ref, wrg_ref, wf_ref, bf_ref, qg_ref, kg_ref, bd_ref,
                  cw_ref, cb_ref, wa_ref, ba_ref, wx_ref, bx_ref, lam_ref, cprev_ref, h0_ref,
                  q_out, k_out, v_out, lf_out, y_out, cst_out, h_out,
                  ext_ref, hc_ref, *, tm, tps, dil, hp):
    i = pl.program_id(0)
    first = (i % tps) == 0
    xn = _rms(x_ref[...], g_ref[...]).astype(BF16)

    bd = bd_ref[...]

    def head_norm(z, gain):
        z2 = z * z
        hi = z2.astype(BF16)
        lo = (z2 - hi.astype(F32)).astype(BF16)
        ms = _dot(hi, bd) + _dot(lo, bd)
        return z * lax.rsqrt(ms + EPS) * gain

    qk = _dot(xn, wqk_ref[...])
    q = head_norm(qk[:, :A_WIDTH], qg_ref[...])
    k = head_norm(qk[:, A_WIDTH:], kg_ref[...])
    q_out[...] = (q * (HEAD_DIM ** -0.5)).astype(BF16)
    k_out[...] = k
    v_out[...] = _dot(xn, wv_ref[...])

    f = _dot(xn, wf_ref[...]) + bf_ref[...]
    lf = jnp.minimum(f, 0.0) - jnp.log1p(jnp.exp(-jnp.abs(f)))
    lf_out[...] = lf[:, :FOX_HEADS]

    rg = _dot(xn, wrg_ref[...])
    xr = rg[:, :B_WIDTH]
    gate = rg[:, B_WIDTH:]

    @pl.when(first)
    def _():
        ext_ref[0:hp, :] = cprev_ref[0]
        hc_ref[...] = h0_ref[0]

    ext_ref[hp:hp + tm, :] = xr
    xc = _conv_from_ext(ext_ref, cw_ref, cb_ref[...], RG_CONV, dil, tm, hp)
    tail = ext_ref[tm:tm + hp, :]
    ext_ref[0:hp, :] = tail
    cst_out[0] = tail

    xcb = xc.astype(BF16)
    r = jax.nn.sigmoid(_dot(xcb, wa_ref[...]) + ba_ref[...])
    gi = jax.nn.sigmoid(_dot(xcb, wx_ref[...]) + bx_ref[...])
    log_a = -RG_C * r * _softplus(-lam_ref[...])
    a = jnp.exp(log_a)
    inp = jnp.sqrt(-jnp.tanh(log_a) * (a * a + 1.0)) * gi * xc

    rows = lax.broadcasted_iota(jnp.int32, (tm, 1), 0)
    s = dil
    while s < tm:
        valid = rows >= s
        a_sh = jnp.where(valid, pltpu.roll(a, s, 0), 1.0)
        h_sh = jnp.where(valid, pltpu.roll(inp, s, 0), 0.0)
        inp = a * h_sh + inp
        a = a * a_sh
        s *= 2

    hc = hc_ref[0:dil, :]
    if dil == 1:
        hc_rows = jnp.broadcast_to(hc, (tm, B_WIDTH))
    else:
        hc_rows = jnp.concatenate([hc] * (tm // dil), axis=0)
    h = inp + a * hc_rows
    hc_ref[0:dil, :] = h[tm - dil:tm, :]
    h_out[0] = hc_ref[...]
    y_out[...] = (h * _gelu(gate)).astype(BF16)


def _ab_in_call(x, geom, w, cprev, h0):
    tm, hp = geom.tm, geom.halo(RG_CONV)
    dilp = _round_up(geom.dil, SUBLANES)
    rows = geom.rows
    tile = lambda n: pl.BlockSpec((tm, n), lambda i: (i, 0))
    seq3 = lambda r, n: pl.BlockSpec((1, r, n), lambda i: (i // geom.tps, 0, 0))
    consts = [w["g"], w["wqk"], w["wv"], w["wrg"], w["wf"], w["bf"], w["qg"], w["kg"], w["bd"],
              w["cw"], w["cb"], w["wa"], w["ba"], w["wx"], w["bx"], w["lam"]]
    kern = functools.partial(_ab_in_kernel, tm=tm, tps=geom.tps, dil=geom.dil, hp=hp)
    return pl.pallas_call(
        kern,
        grid=(geom.n_tiles,),
        in_specs=[tile(D_MODEL)] + [_full_spec(c.shape) for c in consts]
        + [seq3(hp, B_WIDTH), seq3(dilp, B_WIDTH)],
        out_specs=[tile(A_WIDTH), tile(A_WIDTH), tile(A_WIDTH), tile(FOX_HEADS), tile(B_WIDTH),
                   seq3(hp, B_WIDTH), seq3(dilp, B_WIDTH)],
        out_shape=[jax.ShapeDtypeStruct((rows, A_WIDTH), BF16),
                   jax.ShapeDtypeStruct((rows, A_WIDTH), F32),
                   jax.ShapeDtypeStruct((rows, A_WIDTH), F32),
                   jax.ShapeDtypeStruct((rows, FOX_HEADS), F32),
                   jax.ShapeDtypeStruct((rows, B_WIDTH), BF16),
                   jax.ShapeDtypeStruct((geom.n_seq, hp, B_WIDTH), F32),
                   jax.ShapeDtypeStruct((geom.n_seq, dilp, B_WIDTH), F32)],
        scratch_shapes=[pltpu.VMEM((hp + tm, B_WIDTH), F32), pltpu.VMEM((dilp, B_WIDTH), F32)],
        compiler_params=_params(1),
    )(x, *consts, cprev, h0)


def _ab_out_kernel(x_ref, a_ref, y_ref, wa_ref, wy_ref, o_ref):
    o_ref[...] = x_ref[...] + _dot(a_ref[...], wa_ref[...]) + _dot(y_ref[...], wy_ref[...])


def _ab_out_call(x, attn, yrg, w_attn, w_rg, geom):
    tm = geom.tm
    tile = lambda n: pl.BlockSpec((tm, n), lambda i: (i, 0))
    return pl.pallas_call(
        _ab_out_kernel,
        grid=(geom.n_tiles,),
        in_specs=[tile(D_MODEL), tile(A_WIDTH), tile(B_WIDTH),
                  _full_spec(w_attn.shape), _full_spec(w_rg.shape)],
        out_specs=tile(D_MODEL),
        out_shape=jax.ShapeDtypeStruct((geom.rows, D_MODEL), F32),
        compiler_params=_params(1),
    )(x, attn, yrg, w_attn, w_rg)


def _cumsum_kernel(x_ref, o_ref, *, n):
    x = x_ref[...]
    lane = lax.broadcasted_iota(jnp.int32, x.shape, 1)
    s = 1
    while s < n:
        x = x + jnp.where(lane >= s, pltpu.roll(x, s, 1), 0.0)
        s *= 2
    o_ref[...] = x


def _cumsum_call(x):
    return pl.pallas_call(
        functools.partial(_cumsum_kernel, n=x.shape[1]),
        grid=(1,),
        in_specs=[_full_spec(x.shape)],
        out_specs=_full_spec(x.shape),
        out_shape=jax.ShapeDtypeStruct(x.shape, F32),
        compiler_params=_params(1),
    )(x)


def _attn_prompt_kernel(q_ref, k_ref, v_ref, cq_ref, ck_ref, o_ref, *, tq):
    qi = pl.program_id(2)
    q = q_ref[0, 0]
    cq = cq_ref[0, 0]
    row = lax.broadcasted_iota(jnp.int32, (tq, tq), 0)
    col = lax.broadcasted_iota(jnp.int32, (tq, tq), 1)

    def step(j, carry, diagonal):
        m, l, acc = carry
        k0 = pl.multiple_of(j * tq, tq)
        kb = k_ref[0, 0, pl.ds(k0, tq), :]
        vb = v_ref[0, 0, pl.ds(k0, tq), :]
        ck = ck_ref[0, 0, :, pl.ds(k0, tq)]
        s = _dot_nt(q, kb) + (cq - ck)
        if diagonal:
            s = jnp.where(col <= row, s, NEG)
        m_new = jnp.maximum(m, jnp.max(s, axis=1, keepdims=True))
        alpha = jnp.exp(m - m_new)
        p = jnp.exp(s - m_new)
        l = alpha * l + jnp.sum(p, axis=1, keepdims=True)
        acc = alpha * acc + _dot(p.astype(BF16), vb)
        return m_new, l, acc

    init = (jnp.full((tq, 1), NEG, F32), jnp.zeros((tq, 1), F32), jnp.zeros((tq, HEAD_DIM), F32))
    carry = lax.fori_loop(0, qi, lambda j, c: step(j, c, False), init)
    _, l, acc = step(qi, carry, True)
    o_ref[0, 0] = (acc / l).astype(BF16)


def _attn_prompt_call(q, k, v, c_col, c_row, tq):
    bsz, nh, t, dh = q.shape
    return pl.pallas_call(
        functools.partial(_attn_prompt_kernel, tq=tq),
        grid=(bsz, nh, t // tq),
        in_specs=[pl.BlockSpec((1, 1, tq, dh), lambda b, h, i: (b, h, i, 0)),
                  pl.BlockSpec((1, 1, t, dh), lambda b, h, i: (b, h, 0, 0)),
                  pl.BlockSpec((1, 1, t, dh), lambda b, h, i: (b, h, 0, 0)),
                  pl.BlockSpec((1, 1, tq, 1), lambda b, h, i: (b, h, i, 0)),
                  pl.BlockSpec((1, 1, 1, t), lambda b, h, i: (b, h, 0, 0))],
        out_specs=pl.BlockSpec((1, 1, tq, dh), lambda b, h, i: (b, h, i, 0)),
        out_shape=jax.ShapeDtypeStruct((bsz, nh, t, dh), BF16),
        compiler_params=_params(3),
    )(q, k, v, c_col, c_row)


def _attn_sample_kernel(pt_ref, q_ref, kn_ref, vn_ref, lfn_ref, *refs, n_pages, page, dt):
    del pt_ref
    k_refs = refs[:n_pages]
    v_refs = refs[n_pages:2 * n_pages]
    lf_refs = refs[2 * n_pages:3 * n_pages]
    o_ref = refs[3 * n_pages]
    nh = FOX_HEADS
    nr = nh * dt
    past = n_pages * page

    q = q_ref[0]
    qt = jnp.concatenate([q] * nh, axis=0)
    row_head = lax.broadcasted_iota(jnp.int32, (nr, A_WIDTH), 0) // dt
    lane_head = lax.broadcasted_iota(jnp.int32, (nr, A_WIDTH), 1) // HEAD_DIM
    own = row_head == lane_head
    qbd = jnp.where(own, qt, jnp.zeros_like(qt))

    def per_head_rows(x):
        return jnp.concatenate(
            [jnp.broadcast_to(x[h:h + 1, :], (dt, x.shape[1])) for h in range(nh)], axis=0)

    lft = jnp.concatenate([r[0] for r in lf_refs], axis=1)
    lane = lax.broadcasted_iota(jnp.int32, lft.shape, 1)
    suf = lft
    s = 1
    while s < past:
        suf = suf + jnp.where(lane < past - s, pltpu.roll(suf, past - s, 1), 0.0)
        s *= 2
    suf = per_head_rows(suf - lft)

    lnew = per_head_rows(lfn_ref[0])
    colq = lax.broadcasted_iota(jnp.int32, (nr, dt), 1)
    tq = lax.broadcasted_iota(jnp.int32, (nr, dt), 0) % dt
    causal = colq <= tq
    nq = jnp.sum(jnp.where(causal, lnew, 0.0), axis=1, keepdims=True)
    g = jnp.zeros((nr, dt), F32)
    for l in range(dt):
        g = g + jnp.where(colq >= l, lnew[:, l:l + 1], 0.0)

    s_past = [_dot_nt(qbd, k_refs[p][0].astype(BF16)) + (suf[:, p * page:(p + 1) * page] + nq)
              for p in range(n_pages)]
    s_new = jnp.where(causal, _dot_nt(qbd, kn_ref[0]) + (nq - g), NEG)

    m = jnp.max(s_new, axis=1, keepdims=True)
    for sp in s_past:
        m = jnp.maximum(m, jnp.max(sp, axis=1, keepdims=True))
    p_new = jnp.exp(s_new - m)
    l = jnp.sum(p_new, axis=1, keepdims=True)
    acc = _dot(p_new.astype(BF16), vn_ref[0])
    for p in range(n_pages):
        pp = jnp.exp(s_past[p] - m)
        l = l + jnp.sum(pp, axis=1, keepdims=True)
        acc = acc + _dot(pp.astype(BF16), v_refs[p][0].astype(BF16))
    acc = jnp.where(own, acc / l, 0.0)
    out = acc[0:dt, :]
    for h in range(1, nh):
        out = out + acc[h * dt:(h + 1) * dt, :]
    o_ref[0] = out.astype(BF16)


def _attn_sample_call(page_table, q, k_new, v_new, lf_new_t, cache_k, cache_v, cache_lf_t, layer,
                      n_pool_pages):
    db, dt, _ = q.shape
    n_pages = page_table.shape[1]
    page = cache_k.shape[1]
    base = layer * n_pool_pages
    new_spec = pl.BlockSpec((1, dt, A_WIDTH), lambda b, pt: (b, 0, 0))

    def page_spec(shape, p):
        return pl.BlockSpec((1,) + shape, lambda b, pt: (base + pt[b, p], 0, 0))

    in_specs = ([new_spec, new_spec, new_spec,
                 pl.BlockSpec((1, FOX_HEADS, dt), lambda b, pt: (b, 0, 0))]
                + [page_spec((page, A_WIDTH), p) for p in range(n_pages)]
                + [page_spec((page, A_WIDTH), p) for p in range(n_pages)]
                + [page_spec((FOX_HEADS, page), p) for p in range(n_pages)])
    grid_spec = pltpu.PrefetchScalarGridSpec(
        num_scalar_prefetch=1, grid=(db,), in_specs=in_specs, out_specs=new_spec)
    return pl.pallas_call(
        functools.partial(_attn_sample_kernel, n_pages=n_pages, page=page, dt=dt),
        grid_spec=grid_spec,
        out_shape=jax.ShapeDtypeStruct((db, dt, A_WIDTH), BF16),
        compiler_params=_params(1),
    )(page_table, q, k_new, v_new, lf_new_t,
      *([cache_k] * n_pages), *([cache_v] * n_pages), *([cache_lf_t] * n_pages))


def _pool_kernel(xf_ref, xg_ref, gain_ref, prev_ref, w_ref, scale_ref, o_ref, st_ref,
                 inv_ref, ext_ref, *, tm, tps, dil, hp, pos0):
    i = pl.program_id(0)
    g = pl.program_id(1)
    first = (i % tps) == 0

    @pl.when(g == 0)
    def _():
        xf = xf_ref[...]
        inv_ref[...] = lax.rsqrt(jnp.mean(xf * xf, axis=-1, keepdims=True) + EPS)

    xg = xg_ref[...]
    xn = xg * inv_ref[...] * gain_ref[...]
    rows = lax.broadcasted_iota(jnp.int32, (tm, 1), 0) + (i % tps) * tm
    pos = lax.div(rows, jnp.int32(dil)) + pos0

    for gi, window in enumerate(POOL_WINDOWS):
        @pl.when(g == gi)
        def _(gi=gi, window=window):
            ext = ext_ref.at[gi]

            @pl.when(first)
            def _():
                ext[0:hp, :] = prev_ref[0]

            ext[hp:hp + tm, :] = xn
            total = xn
            for j in range(1, window):
                total = total + ext[hp - j * dil:hp - j * dil + tm, :]
            cnt = jnp.minimum(pos + 1, window).astype(F32)
            diff = total / cnt - xn
            y = _dot(diff.astype(BF16), w_ref[0]) * scale_ref[...]
            o_ref[...] = xg + y
            tail = ext[tm:tm + hp, :]
            ext[0:hp, :] = tail
            st_ref[0] = tail


def _pool_call(x, geom, gain, prev, w, scale, pos0):
    tm, hp = geom.tm, geom.halo(POOL_BUF + 1)
    ng = len(POOL_WINDOWS)
    kern = functools.partial(_pool_kernel, tm=tm, tps=geom.tps, dil=geom.dil, hp=hp, pos0=pos0)
    return pl.pallas_call(
        kern,
        grid=(geom.n_tiles, ng),
        in_specs=[pl.BlockSpec((tm, D_MODEL), lambda i, g: (i, 0)),
                  pl.BlockSpec((tm, POOL_GD), lambda i, g: (i, g)),
                  pl.BlockSpec((1, POOL_GD), lambda i, g: (0, g)),
                  pl.BlockSpec((1, hp, POOL_GD), lambda i, g: (i // geom.tps, 0, g)),
                  pl.BlockSpec((1, POOL_GD, POOL_GD), lambda i, g: (g, 0, 0)),
                  pl.BlockSpec((1, POOL_GD), lambda i, g: (0, g))],
        out_specs=[pl.BlockSpec((tm, POOL_GD), lambda i, g: (i, g)),
                   pl.BlockSpec((1, hp, POOL_GD), lambda i, g: (i, 0, g))],
        out_shape=[jax.ShapeDtypeStruct((geom.rows, D_MODEL), F32),
                   jax.ShapeDtypeStruct((geom.n_tiles, hp, D_MODEL), F32)],
        scratch_shapes=[pltpu.VMEM((tm, 1), F32), pltpu.VMEM((ng, hp + tm, POOL_GD), F32)],
        compiler_params=_params(2),
    )(x, x, gain, prev, w, scale)


def _ffn_kernel(x_ref, g_ref, wug_ref, wuv_ref, cwg_ref, cwv_ref, cbg_ref, cbv_ref, pg_ref, pv_ref,
                wd_ref, o_ref, sg_ref, sv_ref, xn_ref, extg_ref, extv_ref, carg_ref, carv_ref,
                *, tm, tps, dil, hp):
    i = pl.program_id(0)
    c = pl.program_id(1)
    first = (i % tps) == 0

    @pl.when(c == 0)
    def _():
        x = x_ref[...]
        xn_ref[...] = _rms(x, g_ref[...]).astype(BF16)
        o_ref[...] = x

    xn = xn_ref[...]

    def half(wu_ref, cw_ref, cb_ref, p_ref, ext_ref, car_ref, s_ref):
        @pl.when(first)
        def _():
            ext_ref[0:hp, :] = p_ref[0]

        @pl.when(jnp.logical_not(first))
        def _():
            ext_ref[0:hp, :] = car_ref[c]

        ext_ref[hp:hp + tm, :] = _dot(xn, wu_ref[...])
        y = _conv_from_ext(ext_ref, cw_ref, cb_ref[...], FFN_CONV, dil, tm, hp)
        tail = ext_ref[tm:tm + hp, :]
        car_ref[c] = tail
        s_ref[0] = tail
        return y

    yg = half(wug_ref, cwg_ref, cbg_ref, pg_ref, extg_ref, carg_ref, sg_ref)
    yv = half(wuv_ref, cwv_ref, cbv_ref, pv_ref, extv_ref, carv_ref, sv_ref)
    h = (_gelu(yg) * yv).astype(BF16)
    o_ref[...] += _dot(h, wd_ref[...])


def _ffn_call(x, geom, gain, w_up, conv_w, conv_b, prev, w_down, ck):
    tm, hp = geom.tm, geom.halo(FFN_CONV)
    nck = D_FF // ck
    tps = geom.tps
    kern = functools.partial(_ffn_kernel, tm=tm, tps=tps, dil=geom.dil, hp=hp)
    col = lambda r, off: pl.BlockSpec((r, ck), lambda i, c: (0, off + c))
    st_in = lambda off: pl.BlockSpec((1, hp, ck), lambda i, c: (i // tps, 0, off + c))
    st_out = pl.BlockSpec((1, hp, ck), lambda i, c: (i, 0, c))
    return pl.pallas_call(
        kern,
        grid=(geom.n_tiles, nck),
        in_specs=[pl.BlockSpec((tm, D_MODEL), lambda i, c: (i, 0)),
                  _full_spec(gain.shape),
                  col(D_MODEL, 0), col(D_MODEL, nck),
                  col(FFN_CONV, 0), col(FFN_CONV, nck),
                  col(1, 0), col(1, nck),
                  st_in(0), st_in(nck),
                  pl.BlockSpec((ck, D_MODEL), lambda i, c: (c, 0))],
        out_specs=[pl.BlockSpec((tm, D_MODEL), lambda i, c: (i, 0)), st_out, st_out],
        out_shape=[jax.ShapeDtypeStruct((geom.rows, D_MODEL), F32),
                   jax.ShapeDtypeStruct((geom.n_tiles, hp, D_FF), F32),
                   jax.ShapeDtypeStruct((geom.n_tiles, hp, D_FF), F32)],
        scratch_shapes=[pltpu.VMEM((tm, D_MODEL), BF16),
                        pltpu.VMEM((hp + tm, ck), F32), pltpu.VMEM((hp + tm, ck), F32),
                        pltpu.VMEM((nck, hp, ck), F32), pltpu.VMEM((nck, hp, ck), F32)],
        compiler_params=_params(2),
    )(x, gain, w_up, w_up, conv_w, conv_w, conv_b, conv_b, prev, prev, w_down)


def _state_to_halo(state, geom, hp):
    n, w1, ch = state.shape
    if geom.dil == 1:
        rows = state
    else:
        rows = state.transpose(1, 0, 2).reshape(1, w1 * n, ch)
    return jnp.pad(rows, ((0, 0), (hp - rows.shape[1], 0), (0, 0)))


def _halo_to_state(halo, geom, n, w1):
    ch = halo.shape[-1]
    if halo.shape[0] != geom.n_seq:
        halo = halo[geom.tps - 1::geom.tps]
    if geom.dil == 1:
        return halo[:, halo.shape[1] - w1:, :]
    return halo[0, halo.shape[1] - w1 * n:, :].reshape(w1, n, ch).transpose(1, 0, 2)


def _block_diag(w):
    nb, bi, bj = w.shape
    eye = jnp.eye(nb, dtype=w.dtype)
    return (eye[:, None, :, None] * w[:, :, None, :]).reshape(nb * bi, nb * bj)


def _trunk(x_rows, geom, n, pos0, p, paged, rg_h, rg_conv, pool_buf, ffn_buf, ffn_ck):
    depth = p["norm_mix"].shape[0]
    ks, vs, lfs, hs, cs, pbs, fbs = [], [], [], [], [], [], []
    dil = geom.dil
    t_steps = geom.rows // n
    dilp = _round_up(dil, SUBLANES)
    x = x_rows
    for layer in range(depth):
        li = layer // 2
        if layer % 2 == 0:
            w_in = p["ab_w_in"][li]
            a3 = 3 * A_WIDTH
            w = {
                "g": p["norm_mix"][layer][None, :],
                "wqk": w_in[:, :2 * A_WIDTH].astype(BF16),
                "wv": w_in[:, 2 * A_WIDTH:a3].astype(BF16),
                "wf": jnp.pad(w_in[:, a3:a3 + FOX_HEADS], ((0, 0), (0, F_PAD - FOX_HEADS))).astype(BF16),
                "wrg": w_in[:, a3 + FOX_HEADS:].astype(BF16),
                "bf": jnp.pad(p["ab_b_f"][li], (0, F_PAD - FOX_HEADS))[None, :],
                "qg": jnp.tile(p["ab_q_gain"][li], FOX_HEADS)[None, :],
                "kg": jnp.tile(p["ab_k_gain"][li], FOX_HEADS)[None, :],
                "bd": _block_diag(jnp.full((FOX_HEADS, HEAD_DIM, HEAD_DIM), 1.0 / HEAD_DIM, F32)).astype(BF16),
                "cw": p["ab_conv_w"][li], "cb": p["ab_conv_b"][li][None, :],
                "wa": _block_diag(p["ab_w_a"][li]).astype(BF16), "ba": p["ab_b_a"][li][None, :],
                "wx": _block_diag(p["ab_w_x"][li]).astype(BF16), "bx": p["ab_b_x"][li][None, :],
                "lam": p["ab_lambda"][li][None, :],
            }
            hp = geom.halo(RG_CONV)
            cprev = _state_to_halo(rg_conv[li], geom, hp)
            if dil == 1:
                h0 = jnp.pad(rg_h[li][:, None, :], ((0, 0), (0, dilp - 1), (0, 0)))
            else:
                h0 = jnp.pad(rg_h[li][None], ((0, 0), (0, dilp - dil), (0, 0)))
            q, k, v, lf, yrg, cst, hl = _ab_in_call(x, geom, w, cprev, h0)

            if paged is None:
                def heads(z):
                    return z.reshape(n, t_steps, FOX_HEADS, HEAD_DIM).transpose(0, 2, 1, 3)
                lf_t = lf.reshape(n, t_steps, FOX_HEADS).transpose(0, 2, 1)
                c = _cumsum_call(lf_t.reshape(n * FOX_HEADS, t_steps))
                attn = _attn_prompt_call(
                    heads(q), heads(k.astype(BF16)), heads(v.astype(BF16)),
                    c.reshape(n, FOX_HEADS, t_steps, 1), c.reshape(n, FOX_HEADS, 1, t_steps),
                    min(512, t_steps))
                attn = attn.transpose(0, 2, 1, 3).reshape(geom.rows, A_WIDTH)
                ks.append(k.reshape(n, t_steps, FOX_HEADS, HEAD_DIM))
                vs.append(v.reshape(n, t_steps, FOX_HEADS, HEAD_DIM))
                lfs.append(lf.reshape(n, t_steps, FOX_HEADS))
                hs.append(hl[:, 0, :])
            else:
                ck2, cv2, clf_t, page_table, n_pool_pages = paged

                def batch_major(z):
                    return z.reshape(t_steps, n, z.shape[-1]).transpose(1, 0, 2)
                k_bm, v_bm, lf_bm = batch_major(k), batch_major(v), batch_major(lf)
                attn = _attn_sample_call(page_table, batch_major(q), k_bm.astype(BF16), v_bm.astype(BF16),
                                         lf_bm.transpose(0, 2, 1), ck2, cv2, clf_t, li, n_pool_pages)
                attn = attn.transpose(1, 0, 2).reshape(geom.rows, A_WIDTH)
                ks.append(k_bm.reshape(n, t_steps, FOX_HEADS, HEAD_DIM))
                vs.append(v_bm.reshape(n, t_steps, FOX_HEADS, HEAD_DIM))
                lfs.append(lf_bm)
                hs.append(hl[0, :dil, :])
            cs.append(_halo_to_state(cst, geom, n, RG_CONV - 1))
            w_out = p["ab_w_out"][li].astype(BF16)
            x = _ab_out_call(x, attn, yrg, w_out[:A_WIDTH], w_out[A_WIDTH:], geom)
        else:
            hp = geom.halo(POOL_BUF + 1)
            prev = _state_to_halo(pool_buf[li], geom, hp)
            x, st = _pool_call(x, geom, p["norm_mix"][layer][None, :], prev,
                               p["pool_w"][li].astype(BF16), p["pool_scale"][li][None, :], pos0)
            pbs.append(_halo_to_state(st, geom, n, POOL_BUF))
        hp = geom.halo(FFN_CONV)
        prev = _state_to_halo(ffn_buf[layer], geom, hp)
        x, sg, sv = _ffn_call(x, geom, p["norm_ffn"][layer][None, :], p["ffn_w_up"][layer].astype(BF16),
                              p["ffn_conv_w"][layer], p["ffn_conv_b"][layer][None, :], prev,
                              p["ffn_w_down"][layer].astype(BF16), ffn_ck)
        fbs.append(_halo_to_state(jnp.concatenate([sg, sv], axis=-1), geom, n, FFN_CONV - 1))
    return x, (jnp.stack(ks), jnp.stack(vs), jnp.stack(lfs), jnp.stack(hs), jnp.stack(cs),
               jnp.stack(pbs), jnp.stack(fbs))


def kernel(x_prompt, x_sample, cache_k, cache_v, cache_logf, state_rg_h, state_rg_conv, state_pool, state_ffn_conv, page_table, norm_mix, norm_ffn, ab_w_in, ab_b_f, ab_q_gain, ab_k_gain, ab_conv_w, ab_conv_b, ab_w_a, ab_b_a, ab_w_x, ab_b_x, ab_lambda, ab_w_out, pool_w, pool_scale, ffn_w_up, ffn_conv_w, ffn_conv_b, ffn_w_down):
    p = {
        "norm_mix": norm_mix, "norm_ffn": norm_ffn,
        "ab_w_in": ab_w_in, "ab_b_f": ab_b_f, "ab_q_gain": ab_q_gain, "ab_k_gain": ab_k_gain,
        "ab_conv_w": ab_conv_w, "ab_conv_b": ab_conv_b, "ab_w_a": ab_w_a, "ab_b_a": ab_b_a,
        "ab_w_x": ab_w_x, "ab_b_x": ab_b_x, "ab_lambda": ab_lambda, "ab_w_out": ab_w_out,
        "pool_w": pool_w, "pool_scale": pool_scale,
        "ffn_w_up": ffn_w_up, "ffn_conv_w": ffn_conv_w, "ffn_conv_b": ffn_conv_b, "ffn_w_down": ffn_w_down,
    }
    depth = norm_mix.shape[0]
    n_ab, n_pool = (depth + 1) // 2, depth // 2

    bsz, t, _ = x_prompt.shape
    geom_p = _Geom(bsz, t, 1, 512)
    y_p, st_p = _trunk(
        x_prompt.reshape(bsz * t, D_MODEL), geom_p, bsz, 0, p, None,
        jnp.zeros((n_ab, bsz, B_WIDTH), F32), jnp.zeros((n_ab, bsz, RG_CONV - 1, B_WIDTH), F32),
        jnp.zeros((n_pool, bsz, POOL_BUF, D_MODEL), F32),
        jnp.zeros((depth, bsz, FFN_CONV - 1, 2 * D_FF), F32), 1024)
    y_prompt = y_p.reshape(bsz, t, D_MODEL)

    db, dt, _ = x_sample.shape
    n_pool_pages, page = cache_k.shape[1], cache_k.shape[2]
    past_len = page_table.shape[1] * page
    geom_s = _Geom(1, dt * db, db, 512)
    paged = (cache_k.reshape(n_ab * n_pool_pages, page, A_WIDTH),
             cache_v.reshape(n_ab * n_pool_pages, page, A_WIDTH),
             cache_logf.transpose(0, 1, 3, 2).reshape(n_ab * n_pool_pages, FOX_HEADS, page),
             page_table, n_pool_pages)
    y_s, st_s = _trunk(
        x_sample.transpose(1, 0, 2).reshape(dt * db, D_MODEL), geom_s, db, past_len, p, paged,
        state_rg_h, state_rg_conv, state_pool, state_ffn_conv, 512)
    y_sample = y_s.reshape(dt, db, D_MODEL).transpose(1, 0, 2)
    return (y_prompt, y_sample) + st_p + st_s
```

```python
import functools

import jax
import jax.numpy as jnp
from jax import lax
from jax.experimental import pallas as pl
from jax.experimental.pallas import tpu as pltpu

D_MODEL = 1024
A_WIDTH = 512
B_WIDTH = 512
HEAD_DIM = 64
FOX_HEADS = 8
RG_CONV = 4
RG_C = 8.0
POOL_WINDOWS = (2, 4, 8, 16)
POOL_GD = 256
POOL_BUF = 15
D_FF = 3072
FFN_CONV = 3
EPS = 1e-6
NEG = -1e30
F_PAD = 128
AUG = 128
LOG2E = 1.4426950408889634
SUBLANES = 8
F32 = jnp.float32
BF16 = jnp.bfloat16
VMEM_LIMIT_BYTES = 56 * 1024 * 1024


def _round_up(x, m):
    return -(-x // m) * m


def _full_spec(shape):
    return pl.BlockSpec(shape, lambda *_: (0,) * len(shape))


def _params(n_axes):
    return pltpu.CompilerParams(dimension_semantics=("arbitrary",) * n_axes,
                                vmem_limit_bytes=VMEM_LIMIT_BYTES)


def _dot(a, b):
    return jnp.dot(a, b, preferred_element_type=F32)


def _dot_nt(a, b):
    return lax.dot_general(a, b, (((1,), (1,)), ((), ())), preferred_element_type=F32)


def _rms(x, gain):
    y = x * lax.rsqrt(jnp.mean(x * x, axis=-1, keepdims=True) + EPS)
    return y * gain


def _gelu(x):
    return 0.5 * x * (1.0 + jnp.tanh(0.7978845608028654 * (x + 0.044715 * (x * x * x))))


def _softplus(x):
    return jnp.maximum(x, 0.0) + jnp.log1p(jnp.exp(-jnp.abs(x)))


class _Geom:
    def __init__(self, n_seq, rows_per_seq, dil, tile):
        self.n_seq, self.rows_per_seq, self.dil = n_seq, rows_per_seq, dil
        self.tm = min(tile, rows_per_seq)
        assert rows_per_seq % self.tm == 0 and self.tm % dil == 0 and self.tm % SUBLANES == 0
        self.tps = rows_per_seq // self.tm
        self.rows = n_seq * rows_per_seq
        self.n_tiles = self.rows // self.tm

    def halo(self, width):
        return _round_up((width - 1) * self.dil, SUBLANES)


def _conv_from_ext(ext_ref, w_ref, bias, width, dil, tm, hp):
    y = bias
    for j in range(width):
        off = hp - (width - 1 - j) * dil
        y = y + ext_ref[off:off + tm, :] * w_ref[j:j + 1, :]
    return y


def _ab_in_kernel(x_ref, g_ref, wqk_ref, wv_ref, wrg_ref, wf_ref, bf_ref, qg_ref, kg_ref, bd_ref,
                  cw_ref, cb_ref, wa_ref, ba_ref, wx_ref, bx_ref, lam_ref, cprev_ref, h0_ref,
                  q_out, k_out, v_out, kt_out, vt_out, lft_out, y_out, cst_out, h_out,
                  ext_ref, hc_ref, *, tm, tps, dil, hp, q_mult):
    i = pl.program_id(0)
    first = (i % tps) == 0
    xn = _rms(x_ref[...], g_ref[...]).astype(BF16)

    bd = bd_ref[...]

    def head_norm(z, gain):
        z2 = z * z
        hi = z2.astype(BF16)
        lo = (z2 - hi.astype(F32)).astype(BF16)
        ms = _dot(hi, bd) + _dot(lo, bd)
        return z * lax.rsqrt(ms + EPS) * gain

    qk = _dot(xn, wqk_ref[...])
    q = head_norm(qk[:, :A_WIDTH], qg_ref[...])
    k = head_norm(qk[:, A_WIDTH:], kg_ref[...])
    v = _dot(xn, wv_ref[...])
    q_out[...] = (q * q_mult).astype(BF16)
    k_out[...] = k.astype(BF16)
    v_out[...] = v.astype(BF16)

    f = _dot(xn, wf_ref[...]) + bf_ref[...]
    lf = jnp.minimum(f, 0.0) - jnp.log1p(jnp.exp(-jnp.abs(f)))

    unit = tm if dil == 1 else dil
    for u in range(tm // unit):
        rows_u = slice(u * unit, (u + 1) * unit)
        kt_out[u] = k[rows_u, :].T
        vt_out[u] = v[rows_u, :].T
        lft_out[u] = lf[rows_u, :].T[:FOX_HEADS, :]

    rg = _dot(xn, wrg_ref[...])
    xr = rg[:, :B_WIDTH]
    gate = rg[:, B_WIDTH:]

    @pl.when(first)
    def _():
        ext_ref[0:hp, :] = cprev_ref[0]
        hc_ref[...] = h0_ref[0]

    ext_ref[hp:hp + tm, :] = xr
    xc = _conv_from_ext(ext_ref, cw_ref, cb_ref[...], RG_CONV, dil, tm, hp)
    tail = ext_ref[tm:tm + hp, :]
    ext_ref[0:hp, :] = tail
    cst_out[0] = tail

    xcb = xc.astype(BF16)
    r = jax.nn.sigmoid(_dot(xcb, wa_ref[...]) + ba_ref[...])
    gi = jax.nn.sigmoid(_dot(xcb, wx_ref[...]) + bx_ref[...])
    log_a = -RG_C * r * _softplus(-lam_ref[...])
    a = jnp.exp(log_a)
    inp = jnp.sqrt(-jnp.tanh(log_a) * (a * a + 1.0)) * gi * xc

    rows = lax.broadcasted_iota(jnp.int32, (tm, 1), 0)
    s = dil
    while s < tm:
        valid = rows >= s
        a_sh = jnp.where(valid, pltpu.roll(a, s, 0), 1.0)
        h_sh = jnp.where(valid, pltpu.roll(inp, s, 0), 0.0)
        inp = a * h_sh + inp
        a = a * a_sh
        s *= 2

    hc = hc_ref[0:dil, :]
    if dil == 1:
        hc_rows = jnp.broadcast_to(hc, (tm, B_WIDTH))
    else:
        hc_rows = jnp.concatenate([hc] * (tm // dil), axis=0)
    h = inp + a * hc_rows
    hc_ref[0:dil, :] = h[tm - dil:tm, :]
    h_out[0] = hc_ref[...]
    y_out[...] = (h * _gelu(gate)).astype(BF16)


def _ab_in_call(x, geom, w, cprev, h0, q_mult):
    tm, hp, tps = geom.tm, geom.halo(RG_CONV), geom.tps
    dilp = _round_up(geom.dil, SUBLANES)
    rows = geom.rows
    tile = lambda n: pl.BlockSpec((tm, n), lambda i: (i, 0))
    seq3 = lambda r, n: pl.BlockSpec((1, r, n), lambda i: (i // tps, 0, 0))
    if geom.dil == 1:
        t_shape = lambda ch: (geom.n_seq, ch, geom.rows_per_seq)
        t_spec = lambda ch: pl.BlockSpec((1, ch, tm), lambda i: (i // tps, 0, i % tps))
    else:
        t_shape = lambda ch: (rows // geom.dil, ch, geom.dil)
        t_spec = lambda ch: pl.BlockSpec((tm // geom.dil, ch, geom.dil), lambda i: (i, 0, 0))
    consts = [w["g"], w["wqk"], w["wv"], w["wrg"], w["wf"], w["bf"], w["qg"], w["kg"], w["bd"],
              w["cw"], w["cb"], w["wa"], w["ba"], w["wx"], w["bx"], w["lam"]]
    kern = functools.partial(_ab_in_kernel, tm=tm, tps=tps, dil=geom.dil, hp=hp, q_mult=q_mult)
    return pl.pallas_call(
        kern,
        grid=(geom.n_tiles,),
        in_specs=[tile(D_MODEL)] + [_full_spec(c.shape) for c in consts]
        + [seq3(hp, B_WIDTH), seq3(dilp, B_WIDTH)],
        out_specs=[tile(A_WIDTH), tile(A_WIDTH), tile(A_WIDTH),
                   t_spec(A_WIDTH), t_spec(A_WIDTH), t_spec(FOX_HEADS), tile(B_WIDTH),
                   seq3(hp, B_WIDTH), seq3(dilp, B_WIDTH)],
        out_shape=[jax.ShapeDtypeStruct((rows, A_WIDTH), BF16),
                   jax.ShapeDtypeStruct((rows, A_WIDTH), BF16),
                   jax.ShapeDtypeStruct((rows, A_WIDTH), BF16),
                   jax.ShapeDtypeStruct(t_shape(A_WIDTH), F32),
                   jax.ShapeDtypeStruct(t_shape(A_WIDTH), F32),
                   jax.ShapeDtypeStruct(t_shape(FOX_HEADS), F32),
                   jax.ShapeDtypeStruct((rows, B_WIDTH), BF16),
                   jax.ShapeDtypeStruct((geom.n_seq, hp, B_WIDTH), F32),
                   jax.ShapeDtypeStruct((geom.n_seq, dilp, B_WIDTH), F32)],
        scratch_shapes=[pltpu.VMEM((hp + tm, B_WIDTH), F32), pltpu.VMEM((dilp, B_WIDTH), F32)],
        compiler_params=_params(1),
    )(x, *consts, cprev, h0)


def _ab_out_kernel(x_ref, a_ref, y_ref, wa_ref, wy_ref, o_ref, *, attn_transposed):
    if attn_transposed:
        attn = lax.dot_general(a_ref[0], wa_ref[...], (((0,), (0,)), ((), ())),
                               preferred_element_type=F32)
    else:
        attn = _dot(a_ref[...], wa_ref[...])
    o_ref[...] = x_ref[...] + attn + _dot(y_ref[...], wy_ref[...])


def _ab_out_call(x, attn, yrg, w_attn, w_rg, geom):
    tm, tps = geom.tm, geom.tps
    tile = lambda n: pl.BlockSpec((tm, n), lambda i: (i, 0))
    attn_transposed = attn.ndim == 3
    if attn_transposed:
        a_spec = pl.BlockSpec((1, A_WIDTH, tm), lambda i: (i // tps, 0, i % tps))
    else:
        a_spec = tile(A_WIDTH)
    return pl.pallas_call(
        functools.partial(_ab_out_kernel, attn_transposed=attn_transposed),
        grid=(geom.n_tiles,),
        in_specs=[tile(D_MODEL), a_spec, tile(B_WIDTH),
                  _full_spec(w_attn.shape), _full_spec(w_rg.shape)],
        out_specs=tile(D_MODEL),
        out_shape=jax.ShapeDtypeStruct((geom.rows, D_MODEL), F32),
        compiler_params=_params(1),
    )(x, attn, yrg, w_attn, w_rg)


def _cumsum_kernel(x_ref, hi_ref, mid_ref, lo_ref, *, n):
    x = x_ref[...]
    lane = lax.broadcasted_iota(jnp.int32, x.shape, 1)
    s = 1
    while s < n:
        x = x + jnp.where(lane >= s, pltpu.roll(x, s, 1), 0.0)
        s *= 2
    x = x * LOG2E
    hi = x.astype(BF16)
    rest = x - hi.astype(F32)
    mid = rest.astype(BF16)
    hi_ref[...] = hi
    mid_ref[...] = mid
    lo_ref[...] = (rest - mid.astype(F32)).astype(BF16)


def _cumsum_call(x):
    out = jax.ShapeDtypeStruct(x.shape, BF16)
    return pl.pallas_call(
        functools.partial(_cumsum_kernel, n=x.shape[1]),
        grid=(1,),
        in_specs=[_full_spec(x.shape)],
        out_specs=[_full_spec(x.shape)] * 3,
        out_shape=[out] * 3,
        compiler_params=_params(1),
    )(x)


def _attn_prompt_kernel(qt_ref, ka_ref, vt_ref, o_ref, *, tq, sb, depth):
    qi = pl.program_id(2)
    qt = qt_ref[0, 0]
    n_sb = tq // sb
    row = lax.broadcasted_iota(jnp.int32, (sb, tq), 0)
    col = lax.broadcasted_iota(jnp.int32, (sb, tq), 1)

    def key_rows(j, s):
        return pl.ds(pl.multiple_of(j * tq + s * sb, sb), sb)

    def scores(j, s):
        return _dot(ka_ref[0, 0, key_rows(j, s), :], qt)

    def chunk(j, carry, diagonal):
        (m, l, acc), ahead = carry[0], list(carry[1])
        for s in range(n_sb):
            st = ahead.pop(0)
            if s + depth < n_sb:
                ahead.append(scores(j, s + depth))
            elif not diagonal:
                ahead.append(scores(j + 1, s + depth - n_sb))
            if diagonal:
                st = jnp.where(row + s * sb <= col, st, NEG)
            m_new = jnp.maximum(m, jnp.max(st, axis=0, keepdims=True))
            alpha = jnp.exp2(m - m_new)
            p = jnp.exp2(st - m_new)
            l = alpha * l + jnp.sum(p, axis=0, keepdims=True)
            acc = alpha * acc + _dot(vt_ref[0, 0, :, key_rows(j, s)], p.astype(BF16))
            m = m_new
        return (m, l, acc), tuple(ahead)

    state = (jnp.full((1, tq), NEG, F32), jnp.zeros((1, tq), F32), jnp.zeros((HEAD_DIM, tq), F32))
    carry = (state, tuple(scores(0, s) for s in range(depth)))
    carry = lax.fori_loop(0, qi, lambda j, c: chunk(j, c, False), carry)
    (_, l, acc), _ = chunk(qi, carry, True)
    o_ref[0, 0] = (acc / l).astype(BF16)


def _attn_prompt_call(q_aug_t, k_aug, v_t, tq):
    bsz, nh, aug, t = q_aug_t.shape
    sb = min(128, tq)
    depth = min(2, tq // sb)
    return pl.pallas_call(
        functools.partial(_attn_prompt_kernel, tq=tq, sb=sb, depth=depth),
        grid=(bsz, nh, t // tq),
        in_specs=[pl.BlockSpec((1, 1, aug, tq), lambda b, h, i: (b, h, 0, i)),
                  pl.BlockSpec((1, 1, t, aug), lambda b, h, i: (b, h, 0, 0)),
                  pl.BlockSpec((1, 1, HEAD_DIM, t), lambda b, h, i: (b, h, 0, 0))],
        out_specs=pl.BlockSpec((1, 1, HEAD_DIM, tq), lambda b, h, i: (b, h, 0, i)),
        out_shape=jax.ShapeDtypeStruct((bsz, nh, HEAD_DIM, t), BF16),
        compiler_params=_params(3),
    )(q_aug_t, k_aug, v_t)


def _attn_sample_kernel(pt_ref, q_ref, kn_ref, vn_ref, lfn_ref, *refs, n_pages, page, dt):
    del pt_ref
    k_refs = refs[:n_pages]
    v_refs = refs[n_pages:2 * n_pages]
    lf_refs = refs[2 * n_pages:3 * n_pages]
    o_ref = refs[3 * n_pages]
    nh = FOX_HEADS
    past = n_pages * page
    bdot = lambda a, b: lax.dot_general(a, b, (((2,), (1,)), ((0,), (0,))), preferred_element_type=F32)
    bdot_nt = lambda a, b: lax.dot_general(a, b, (((2,), (2,)), ((0,), (0,))), preferred_element_type=F32)

    q3 = q_ref[0]

    lft = jnp.concatenate([r[0, 0] for r in lf_refs], axis=1)
    lane = lax.broadcasted_iota(jnp.int32, lft.shape, 1)
    suf = lft
    s = 1
    while s < past:
        suf = suf + jnp.where(lane < past - s, pltpu.roll(suf, past - s, 1), 0.0)
        s *= 2
    suf = suf - lft
    suf3 = jnp.stack([jnp.broadcast_to(suf[h:h + 1, :], (dt, past)) for h in range(nh)], axis=0)

    lnew = jnp.broadcast_to(lfn_ref[0], (nh, dt, dt))
    colq = lax.broadcasted_iota(jnp.int32, (nh, dt, dt), 2)
    tq = lax.broadcasted_iota(jnp.int32, (nh, dt, dt), 1)
    causal = colq <= tq
    nq = jnp.sum(jnp.where(causal, lnew, 0.0), axis=2, keepdims=True)
    g = jnp.zeros((nh, dt, dt), F32)
    for l in range(dt):
        g = g + jnp.where(colq >= l, lnew[:, :, l:l + 1], 0.0)

    s_past = [bdot(q3, k_refs[p][0, 0].astype(BF16)) + (suf3[:, :, p * page:(p + 1) * page] + nq)
              for p in range(n_pages)]
    s_new = jnp.where(causal, bdot_nt(q3, kn_ref[0]) + (nq - g), NEG)

    m = jnp.max(s_new, axis=2, keepdims=True)
    for sp in s_past:
        m = jnp.maximum(m, jnp.max(sp, axis=2, keepdims=True))
    p_new = jnp.exp(s_new - m)
    l = jnp.sum(p_new, axis=2, keepdims=True)
    acc = bdot(p_new.astype(BF16), vn_ref[0])
    for p in range(n_pages):
        pp = jnp.exp(s_past[p] - m)
        l = l + jnp.sum(pp, axis=2, keepdims=True)
        acc = acc + bdot_nt(pp.astype(BF16), v_refs[p][0, 0].astype(BF16))
    o_ref[0] = (acc / l).astype(BF16)


def _attn_sample_call(page_table, q, k_new, v_new, lf_new, cache_kt, cache_vt, cache_lft, layer):
    db, nh, dt, dh = q.shape
    n_pages = page_table.shape[1]
    page = cache_kt.shape[-1]
    new_spec = pl.BlockSpec((1, nh, dt, dh), lambda b, pt: (b, 0, 0, 0))

    def page_spec(shape, p):
        zeros = (0,) * len(shape)
        return pl.BlockSpec((1, 1) + shape, lambda b, pt: (layer, pt[b, p]) + zeros)

    in_specs = ([new_spec, new_spec, new_spec,
                 pl.BlockSpec((1, nh, 1, dt), lambda b, pt: (b, 0, 0, 0))]
                + [page_spec((nh, dh, page), p) for p in range(n_pages)]
                + [page_spec((nh, dh, page), p) for p in range(n_pages)]
                + [page_spec((nh, page), p) for p in range(n_pages)])
    grid_spec = pltpu.PrefetchScalarGridSpec(
        num_scalar_prefetch=1, grid=(db,), in_specs=in_specs, out_specs=new_spec)
    return pl.pallas_call(
        functools.partial(_attn_sample_kernel, n_pages=n_pages, page=page, dt=dt),
        grid_spec=grid_spec,
        out_shape=jax.ShapeDtypeStruct((db, nh, dt, dh), BF16),
        compiler_params=_params(1),
    )(page_table, q, k_new, v_new, lf_new,
      *([cache_kt] * n_pages), *([cache_vt] * n_pages), *([cache_lft] * n_pages))


def _pool_kernel(xf_ref, xg_ref, gain_ref, prev_ref, w_ref, scale_ref, o_ref, st_ref,
                 inv_ref, ext_ref, *, tm, tps, dil, hp, pos0):
    i = pl.program_id(0)
    g = pl.program_id(1)
    first = (i % tps) == 0

    @pl.when(g == 0)
    def _():
        xf = xf_ref[...]
        inv_ref[...] = lax.rsqrt(jnp.mean(xf * xf, axis=-1, keepdims=True) + EPS)

    xg = xg_ref[...]
    xn = xg * inv_ref[...] * gain_ref[...]
    rows = lax.broadcasted_iota(jnp.int32, (tm, 1), 0) + (i % tps) * tm
    pos = lax.div(rows, jnp.int32(dil)) + pos0

    for gi, window in enumerate(POOL_WINDOWS):
        @pl.when(g == gi)
        def _(gi=gi, window=window):
            ext = ext_ref.at[gi]

            @pl.when(first)
            def _():
                ext[0:hp, :] = prev_ref[0]

            ext[hp:hp + tm, :] = xn
            total = xn
            for j in range(1, window):
                total = total + ext[hp - j * dil:hp - j * dil + tm, :]
            cnt = jnp.minimum(pos + 1, window).astype(F32)
            diff = total / cnt - xn
            y = _dot(diff.astype(BF16), w_ref[0]) * scale_ref[...]
            o_ref[...] = xg + y
            tail = ext[tm:tm + hp, :]
            ext[0:hp, :] = tail
            st_ref[0] = tail


def _pool_call(x, geom, gain, prev, w, scale, pos0):
    tm, hp = geom.tm, geom.halo(POOL_BUF + 1)
    ng = len(POOL_WINDOWS)
    kern = functools.partial(_pool_kernel, tm=tm, tps=geom.tps, dil=geom.dil, hp=hp, pos0=pos0)
    return pl.pallas_call(
        kern,
        grid=(geom.n_tiles, ng),
        in_specs=[pl.BlockSpec((tm, D_MODEL), lambda i, g: (i, 0)),
                  pl.BlockSpec((tm, POOL_GD), lambda i, g: (i, g)),
                  pl.BlockSpec((1, POOL_GD), lambda i, g: (0, g)),
                  pl.BlockSpec((1, hp, POOL_GD), lambda i, g: (i // geom.tps, 0, g)),
                  pl.BlockSpec((1, POOL_GD, POOL_GD), lambda i, g: (g, 0, 0)),
                  pl.BlockSpec((1, POOL_GD), lambda i, g: (0, g))],
        out_specs=[pl.BlockSpec((tm, POOL_GD), lambda i, g: (i, g)),
                   pl.BlockSpec((1, hp, POOL_GD), lambda i, g: (i, 0, g))],
        out_shape=[jax.ShapeDtypeStruct((geom.rows, D_MODEL), F32),
                   jax.ShapeDtypeStruct((geom.n_tiles, hp, D_MODEL), F32)],
        scratch_shapes=[pltpu.VMEM((tm, 1), F32), pltpu.VMEM((ng, hp + tm, POOL_GD), F32)],
        compiler_params=_params(2),
    )(x, x, gain, prev, w, scale)


def _ffn_kernel(x_ref, g_ref, wug_ref, wuv_ref, cwg_ref, cwv_ref, cbg_ref, cbv_ref, pg_ref, pv_ref,
                wd_ref, o_ref, sg_ref, sv_ref, xn_ref, extg_ref, extv_ref, carg_ref, carv_ref,
                *, tm, tps, dil, hp):
    i = pl.program_id(0)
    c = pl.program_id(1)
    first = (i % tps) == 0

    @pl.when(c == 0)
    def _():
        x = x_ref[...]
        xn_ref[...] = _rms(x, g_ref[...]).astype(BF16)
        o_ref[...] = x

    xn = xn_ref[...]

    def half(wu_ref, cw_ref, cb_ref, p_ref, ext_ref, car_ref, s_ref):
        @pl.when(first)
        def _():
            ext_ref[0:hp, :] = p_ref[0]

        @pl.when(jnp.logical_not(first))
        def _():
            ext_ref[0:hp, :] = car_ref[c]

        ext_ref[hp:hp + tm, :] = _dot(xn, wu_ref[...])
        y = _conv_from_ext(ext_ref, cw_ref, cb_ref[...], FFN_CONV, dil, tm, hp)
        tail = ext_ref[tm:tm + hp, :]
        car_ref[c] = tail
        s_ref[0] = tail
        return y

    yg = half(wug_ref, cwg_ref, cbg_ref, pg_ref, extg_ref, carg_ref, sg_ref)
    yv = half(wuv_ref, cwv_ref, cbv_ref, pv_ref, extv_ref, carv_ref, sv_ref)
    h = (_gelu(yg) * yv).astype(BF16)
    o_ref[...] += _dot(h, wd_ref[...])


def _ffn_call(x, geom, gain, w_up, conv_w, conv_b, prev, w_down, ck):
    tm, hp = geom.tm, geom.halo(FFN_CONV)
    nck = D_FF // ck
    tps = geom.tps
    kern = functools.partial(_ffn_kernel, tm=tm, tps=tps, dil=geom.dil, hp=hp)
    col = lambda r, off: pl.BlockSpec((r, ck), lambda i, c: (0, off + c))
    st_in = lambda off: pl.BlockSpec((1, hp, ck), lambda i, c: (i // tps, 0, off + c))
    st_out = pl.BlockSpec((1, hp, ck), lambda i, c: (i, 0, c))
    return pl.pallas_call(
        kern,
        grid=(geom.n_tiles, nck),
        in_specs=[pl.BlockSpec((tm, D_MODEL), lambda i, c: (i, 0)),
                  _full_spec(gain.shape),
                  col(D_MODEL, 0), col(D_MODEL, nck),
                  col(FFN_CONV, 0), col(FFN_CONV, nck),
                  col(1, 0), col(1, nck),
                  st_in(0), st_in(nck),
                  pl.BlockSpec((ck, D_MODEL), lambda i, c: (c, 0))],
        out_specs=[pl.BlockSpec((tm, D_MODEL), lambda i, c: (i, 0)), st_out, st_out],
        out_shape=[jax.ShapeDtypeStruct((geom.rows, D_MODEL), F32),
                   jax.ShapeDtypeStruct((geom.n_tiles, hp, D_FF), F32),
                   jax.ShapeDtypeStruct((geom.n_tiles, hp, D_FF), F32)],
        scratch_shapes=[pltpu.VMEM((tm, D_MODEL), BF16),
                        pltpu.VMEM((hp + tm, ck), F32), pltpu.VMEM((hp + tm, ck), F32),
                        pltpu.VMEM((nck, hp, ck), F32), pltpu.VMEM((nck, hp, ck), F32)],
        compiler_params=_params(2),
    )(x, gain, w_up, w_up, conv_w, conv_w, conv_b, conv_b, prev, prev, w_down)


def _state_to_halo(state, geom, hp):
    n, w1, ch = state.shape
    if geom.dil == 1:
        rows = state
    else:
        rows = state.transpose(1, 0, 2).reshape(1, w1 * n, ch)
    return jnp.pad(rows, ((0, 0), (hp - rows.shape[1], 0), (0, 0)))


def _halo_to_state(halo, geom, n, w1):
    ch = halo.shape[-1]
    if halo.shape[0] != geom.n_seq:
        halo = halo[geom.tps - 1::geom.tps]
    if geom.dil == 1:
        return halo[:, halo.shape[1] - w1:, :]
    return halo[0, halo.shape[1] - w1 * n:, :].reshape(w1, n, ch).transpose(1, 0, 2)


def _block_diag(w):
    nb, bi, bj = w.shape
    eye = jnp.eye(nb, dtype=w.dtype)
    return (eye[:, None, :, None] * w[:, :, None, :]).reshape(nb * bi, nb * bj)


def _trunk(x_rows, geom, n, pos0, p, paged, rg_h, rg_conv, pool_buf, ffn_buf, ffn_ck):
    depth = p["norm_mix"].shape[0]
    ks, vs, lfs, hs, cs, pbs, fbs = [], [], [], [], [], [], []
    dil = geom.dil
    t_steps = geom.rows // n
    dilp = _round_up(dil, SUBLANES)
    x = x_rows
    for layer in range(depth):
        li = layer // 2
        if layer % 2 == 0:
            w_in = p["ab_w_in"][li]
            a3 = 3 * A_WIDTH
            w = {
                "g": p["norm_mix"][layer][None, :],
                "wqk": w_in[:, :2 * A_WIDTH].astype(BF16),
                "wv": w_in[:, 2 * A_WIDTH:a3].astype(BF16),
                "wf": jnp.pad(w_in[:, a3:a3 + FOX_HEADS], ((0, 0), (0, F_PAD - FOX_HEADS))).astype(BF16),
                "wrg": w_in[:, a3 + FOX_HEADS:].astype(BF16),
                "bf": jnp.pad(p["ab_b_f"][li], (0, F_PAD - FOX_HEADS))[None, :],
                "qg": jnp.tile(p["ab_q_gain"][li], FOX_HEADS)[None, :],
                "kg": jnp.tile(p["ab_k_gain"][li], FOX_HEADS)[None, :],
                "bd": _block_diag(jnp.full((FOX_HEADS, HEAD_DIM, HEAD_DIM), 1.0 / HEAD_DIM, F32)).astype(BF16),
                "cw": p["ab_conv_w"][li], "cb": p["ab_conv_b"][li][None, :],
                "wa": _block_diag(p["ab_w_a"][li]).astype(BF16), "ba": p["ab_b_a"][li][None, :],
                "wx": _block_diag(p["ab_w_x"][li]).astype(BF16), "bx": p["ab_b_x"][li][None, :],
                "lam": p["ab_lambda"][li][None, :],
            }
            hp = geom.halo(RG_CONV)
            cprev = _state_to_halo(rg_conv[li], geom, hp)
            if dil == 1:
                h0 = jnp.pad(rg_h[li][:, None, :], ((0, 0), (0, dilp - 1), (0, 0)))
            else:
                h0 = jnp.pad(rg_h[li][None], ((0, 0), (0, dilp - dil), (0, 0)))
            scale = HEAD_DIM ** -0.5
            q_mult = scale * LOG2E if paged is None else scale
            q, k, v, kt, vt, lft, yrg, cst, hl = _ab_in_call(x, geom, w, cprev, h0, q_mult)

            if paged is None:
                parts = _cumsum_call(lft.reshape(n * FOX_HEADS, t_steps))
                c3 = jnp.stack(parts, axis=1).reshape(n, FOX_HEADS, 3, t_steps)
                pad = AUG - HEAD_DIM - 6
                q_t = q.reshape(n, t_steps, FOX_HEADS, HEAD_DIM).transpose(0, 2, 3, 1)
                q_aug_t = jnp.concatenate(
                    [q_t, -jnp.ones_like(c3), c3, jnp.zeros((n, FOX_HEADS, pad, t_steps), BF16)], axis=2)
                k_h = k.reshape(n, t_steps, FOX_HEADS, HEAD_DIM).transpose(0, 2, 1, 3)
                c3_cols = c3.transpose(0, 1, 3, 2)
                k_aug = jnp.concatenate(
                    [k_h, c3_cols, jnp.ones_like(c3_cols), jnp.zeros((n, FOX_HEADS, t_steps, pad), BF16)],
                    axis=3)
                v_t = vt.astype(BF16).reshape(n, FOX_HEADS, HEAD_DIM, t_steps)
                attn = _attn_prompt_call(q_aug_t, k_aug, v_t, min(512, t_steps))
                attn = attn.reshape(n, A_WIDTH, t_steps)
                ks.append(kt.reshape(n, FOX_HEADS, HEAD_DIM, t_steps).transpose(0, 3, 1, 2))
                vs.append(vt.reshape(n, FOX_HEADS, HEAD_DIM, t_steps).transpose(0, 3, 1, 2))
                lfs.append(lft.transpose(0, 2, 1))
                hs.append(hl[:, 0, :])
            else:
                ckt, cvt, clft, page_table = paged

                def heads_bm(z):
                    return z.reshape(t_steps, n, FOX_HEADS, HEAD_DIM).transpose(1, 2, 0, 3)
                lf_new = lft.transpose(2, 1, 0)[:, :, None, :]
                attn = _attn_sample_call(page_table, heads_bm(q), heads_bm(k), heads_bm(v), lf_new,
                                         ckt, cvt, clft, li)
                attn = attn.transpose(2, 0, 1, 3).reshape(geom.rows, A_WIDTH)
                ks.append(kt.reshape(t_steps, FOX_HEADS, HEAD_DIM, n).transpose(3, 0, 1, 2))
                vs.append(vt.reshape(t_steps, FOX_HEADS, HEAD_DIM, n).transpose(3, 0, 1, 2))
                lfs.append(lft.transpose(2, 0, 1))
                hs.append(hl[0, :dil, :])
            cs.append(_halo_to_state(cst, geom, n, RG_CONV - 1))
            w_out = p["ab_w_out"][li].astype(BF16)
            x = _ab_out_call(x, attn, yrg, w_out[:A_WIDTH], w_out[A_WIDTH:], geom)
        else:
            hp = geom.halo(POOL_BUF + 1)
            prev = _state_to_halo(pool_buf[li], geom, hp)
            x, st = _pool_call(x, geom, p["norm_mix"][layer][None, :], prev,
                               p["pool_w"][li].astype(BF16), p["pool_scale"][li][None, :], pos0)
            pbs.append(_halo_to_state(st, geom, n, POOL_BUF))
        hp = geom.halo(FFN_CONV)
        prev = _state_to_halo(ffn_buf[layer], geom, hp)
        x, sg, sv = _ffn_call(x, geom, p["norm_ffn"][layer][None, :], p["ffn_w_up"][layer].astype(BF16),
                              p["ffn_conv_w"][layer], p["ffn_conv_b"][layer][None, :], prev,
                              p["ffn_w_down"][layer].astype(BF16), ffn_ck)
        fbs.append(_halo_to_state(jnp.concatenate([sg, sv], axis=-1), geom, n, FFN_CONV - 1))
    return x, (jnp.stack(ks), jnp.stack(vs), jnp.stack(lfs), jnp.stack(hs), jnp.stack(cs),
               jnp.stack(pbs), jnp.stack(fbs))


def kernel(x_prompt, x_sample, cache_k, cache_v, cache_logf, state_rg_h, state_rg_conv, state_pool, state_ffn_conv, page_table, norm_mix, norm_ffn, ab_w_in, ab_b_f, ab_q_gain, ab_k_gain, ab_conv_w, ab_conv_b, ab_w_a, ab_b_a, ab_w_x, ab_b_x, ab_lambda, ab_w_out, pool_w, pool_scale, ffn_w_up, ffn_conv_w, ffn_conv_b, ffn_w_down):
    p = {
        "norm_mix": norm_mix, "norm_ffn": norm_ffn,
        "ab_w_in": ab_w_in, "ab_b_f": ab_b_f, "ab_q_gain": ab_q_gain, "ab_k_gain": ab_k_gain,
        "ab_conv_w": ab_conv_w, "ab_conv_b": ab_conv_b, "ab_w_a": ab_w_a, "ab_b_a": ab_b_a,
        "ab_w_x": ab_w_x, "ab_b_x": ab_b_x, "ab_lambda": ab_lambda, "ab_w_out": ab_w_out,
        "pool_w": pool_w, "pool_scale": pool_scale,
        "ffn_w_up": ffn_w_up, "ffn_conv_w": ffn_conv_w, "ffn_conv_b": ffn_conv_b, "ffn_w_down": ffn_w_down,
    }
    depth = norm_mix.shape[0]
    n_ab, n_pool = (depth + 1) // 2, depth // 2

    bsz, t, _ = x_prompt.shape
    geom_p = _Geom(bsz, t, 1, 512)
    y_p, st_p = _trunk(
        x_prompt.reshape(bsz * t, D_MODEL), geom_p, bsz, 0, p, None,
        jnp.zeros((n_ab, bsz, B_WIDTH), F32), jnp.zeros((n_ab, bsz, RG_CONV - 1, B_WIDTH), F32),
        jnp.zeros((n_pool, bsz, POOL_BUF, D_MODEL), F32),
        jnp.zeros((depth, bsz, FFN_CONV - 1, 2 * D_FF), F32), 1024)
    y_prompt = y_p.reshape(bsz, t, D_MODEL)

    db, dt, _ = x_sample.shape
    n_pool_pages, page = cache_k.shape[1], cache_k.shape[2]
    past_len = page_table.shape[1] * page
    geom_s = _Geom(1, dt * db, db, 512)
    paged = (cache_k.transpose(0, 1, 3, 4, 2), cache_v.transpose(0, 1, 3, 4, 2),
             cache_logf.transpose(0, 1, 3, 2), page_table)
    y_s, st_s = _trunk(
        x_sample.transpose(1, 0, 2).reshape(dt * db, D_MODEL), geom_s, db, past_len, p, paged,
        state_rg_h, state_rg_conv, state_pool, state_ffn_conv, 512)
    y_sample = y_s.reshape(dt, db, D_MODEL).transpose(1, 0, 2)
    return (y_prompt, y_sample) + st_p + st_s
```

```python
import functools

import jax
import jax.numpy as jnp
from jax import lax
from jax.experimental import pallas as pl
from jax.experimental.pallas import tpu as pltpu

D_MODEL = 1024
A_WIDTH = 512
B_WIDTH = 512
HEAD_DIM = 64
FOX_HEADS = 8
RG_CONV = 4
RG_C = 8.0
POOL_WINDOWS = (2, 4, 8, 16)
POOL_GD = 256
POOL_BUF = 15
D_FF = 3072
FFN_CONV = 3
EPS = 1e-6
NEG = -1e30
F_PAD = 128
GATE_ROWS = 16
LOG2E = 1.4426950408889634
SUBLANES = 8
F32 = jnp.float32
BF16 = jnp.bfloat16
VMEM_LIMIT_BYTES = 56 * 1024 * 1024


def _round_up(x, m):
    return -(-x // m) * m


def _full_spec(shape):
    return pl.BlockSpec(shape, lambda *_: (0,) * len(shape))


def _params(n_axes):
    return pltpu.CompilerParams(dimension_semantics=("arbitrary",) * n_axes,
                                vmem_limit_bytes=VMEM_LIMIT_BYTES)


def _dot(a, b):
    return jnp.dot(a, b, preferred_element_type=F32)


def _dot_nt(a, b):
    return lax.dot_general(a, b, (((1,), (1,)), ((), ())), preferred_element_type=F32)


def _rms(x, gain):
    y = x * lax.rsqrt(jnp.mean(x * x, axis=-1, keepdims=True) + EPS)
    return y * gain


def _gelu(x):
    return 0.5 * x * (1.0 + jnp.tanh(0.7978845608028654 * (x + 0.044715 * (x * x * x))))


def _softplus(x):
    return jnp.maximum(x, 0.0) + jnp.log1p(jnp.exp(-jnp.abs(x)))


class _Geom:
    def __init__(self, n_seq, rows_per_seq, dil, tile):
        self.n_seq, self.rows_per_seq, self.dil = n_seq, rows_per_seq, dil
        self.tm = min(tile, rows_per_seq)
        assert rows_per_seq % self.tm == 0 and self.tm % dil == 0 and self.tm % SUBLANES == 0
        self.tps = rows_per_seq // self.tm
        self.rows = n_seq * rows_per_seq
        self.n_tiles = self.rows // self.tm

    def halo(self, width):
        return _round_up((width - 1) * self.dil, SUBLANES)


def _conv_from_ext(ext_ref, w_ref, bias, width, dil, tm, hp):
    y = bias
    for j in range(width):
        off = hp - (width - 1 - j) * dil
        y = y + ext_ref[off:off + tm, :] * w_ref[j:j + 1, :]
    return y


def _ab_in_kernel(x_ref, g_ref, wqk_ref, wv_ref, wrg_ref, wf_ref, bf_ref, qg_ref, kg_ref, bd_ref,
                  cw_ref, cb_ref, wa_ref, ba_ref, wx_ref, bx_ref, lam_ref, cprev_ref, h0_ref,
                  q_out, k_out, v_out, kt_out, vt_out, lft_out, y_out, cst_out, h_out,
                  ext_ref, hc_ref, *, tm, tps, dil, hp, q_mult):
    i = pl.program_id(0)
    first = (i % tps) == 0
    xn = _rms(x_ref[...], g_ref[...]).astype(BF16)

    bd = bd_ref[...]

    def head_norm(z, gain):
        z2 = z * z
        hi = z2.astype(BF16)
        lo = (z2 - hi.astype(F32)).astype(BF16)
        ms = _dot(hi, bd) + _dot(lo, bd)
        return z * lax.rsqrt(ms + EPS) * gain

    qk = _dot(xn, wqk_ref[...])
    q = head_norm(qk[:, :A_WIDTH], qg_ref[...])
    k = head_norm(qk[:, A_WIDTH:], kg_ref[...])
    v = _dot(xn, wv_ref[...])
    q = q * q_mult

    f = _dot(xn, wf_ref[...]) + bf_ref[...]
    lf = jnp.minimum(f, 0.0) - jnp.log1p(jnp.exp(-jnp.abs(f)))

    unit = tm if dil == 1 else dil
    for u in range(tm // unit):
        rows_u = slice(u * unit, (u + 1) * unit)
        kt = k[rows_u, :].T
        vt = v[rows_u, :].T
        kt_out[u] = kt
        vt_out[u] = vt
        lft_out[u] = lf[rows_u, :].T[:FOX_HEADS, :]
        if dil == 1:
            q_out[u] = q.T.astype(BF16)
            k_out[u] = kt.astype(BF16)
            v_out[u] = vt.astype(BF16)
    if dil != 1:
        q_out[...] = q.astype(BF16)
        k_out[...] = k.astype(BF16)
        v_out[...] = v.astype(BF16)

    rg = _dot(xn, wrg_ref[...])
    xr = rg[:, :B_WIDTH]
    gate = rg[:, B_WIDTH:]

    @pl.when(first)
    def _():
        ext_ref[0:hp, :] = cprev_ref[0]
        hc_ref[...] = h0_ref[0]

    ext_ref[hp:hp + tm, :] = xr
    xc = _conv_from_ext(ext_ref, cw_ref, cb_ref[...], RG_CONV, dil, tm, hp)
    tail = ext_ref[tm:tm + hp, :]
    ext_ref[0:hp, :] = tail
    cst_out[0] = tail

    xcb = xc.astype(BF16)
    r = jax.nn.sigmoid(_dot(xcb, wa_ref[...]) + ba_ref[...])
    gi = jax.nn.sigmoid(_dot(xcb, wx_ref[...]) + bx_ref[...])
    log_a = -RG_C * r * _softplus(-lam_ref[...])
    a = jnp.exp(log_a)
    inp = jnp.sqrt(-jnp.tanh(log_a) * (a * a + 1.0)) * gi * xc

    rows = lax.broadcasted_iota(jnp.int32, (tm, 1), 0)
    s = dil
    while s < tm:
        valid = rows >= s
        a_sh = jnp.where(valid, pltpu.roll(a, s, 0), 1.0)
        h_sh = jnp.where(valid, pltpu.roll(inp, s, 0), 0.0)
        inp = a * h_sh + inp
        a = a * a_sh
        s *= 2

    hc = hc_ref[0:dil, :]
    if dil == 1:
        hc_rows = jnp.broadcast_to(hc, (tm, B_WIDTH))
    else:
        hc_rows = jnp.concatenate([hc] * (tm // dil), axis=0)
    h = inp + a * hc_rows
    hc_ref[0:dil, :] = h[tm - dil:tm, :]
    h_out[0] = hc_ref[...]
    y_out[...] = (h * _gelu(gate)).astype(BF16)


def _ab_in_call(x, geom, w, cprev, h0, q_mult):
    tm, hp, tps = geom.tm, geom.halo(RG_CONV), geom.tps
    dilp = _round_up(geom.dil, SUBLANES)
    rows = geom.rows
    tile = lambda n: pl.BlockSpec((tm, n), lambda i: (i, 0))
    seq3 = lambda r, n: pl.BlockSpec((1, r, n), lambda i: (i // tps, 0, 0))
    if geom.dil == 1:
        t_shape = lambda ch: (geom.n_seq, ch, geom.rows_per_seq)
        t_spec = lambda ch: pl.BlockSpec((1, ch, tm), lambda i: (i // tps, 0, i % tps))
        op_spec, op_shape = t_spec(A_WIDTH), jax.ShapeDtypeStruct(t_shape(A_WIDTH), BF16)
    else:
        t_shape = lambda ch: (rows // geom.dil, ch, geom.dil)
        t_spec = lambda ch: pl.BlockSpec((tm // geom.dil, ch, geom.dil), lambda i: (i, 0, 0))
        op_spec, op_shape = tile(A_WIDTH), jax.ShapeDtypeStruct((rows, A_WIDTH), BF16)
    consts = [w["g"], w["wqk"], w["wv"], w["wrg"], w["wf"], w["bf"], w["qg"], w["kg"], w["bd"],
              w["cw"], w["cb"], w["wa"], w["ba"], w["wx"], w["bx"], w["lam"]]
    kern = functools.partial(_ab_in_kernel, tm=tm, tps=tps, dil=geom.dil, hp=hp, q_mult=q_mult)
    return pl.pallas_call(
        kern,
        grid=(geom.n_tiles,),
        in_specs=[tile(D_MODEL)] + [_full_spec(c.shape) for c in consts]
        + [seq3(hp, B_WIDTH), seq3(dilp, B_WIDTH)],
        out_specs=[op_spec, op_spec, op_spec,
                   t_spec(A_WIDTH), t_spec(A_WIDTH), t_spec(FOX_HEADS), tile(B_WIDTH),
                   seq3(hp, B_WIDTH), seq3(dilp, B_WIDTH)],
        out_shape=[op_shape, op_shape, op_shape,
                   jax.ShapeDtypeStruct(t_shape(A_WIDTH), F32),
                   jax.ShapeDtypeStruct(t_shape(A_WIDTH), F32),
                   jax.ShapeDtypeStruct(t_shape(FOX_HEADS), F32),
                   jax.ShapeDtypeStruct((rows, B_WIDTH), BF16),
                   jax.ShapeDtypeStruct((geom.n_seq, hp, B_WIDTH), F32),
                   jax.ShapeDtypeStruct((geom.n_seq, dilp, B_WIDTH), F32)],
        scratch_shapes=[pltpu.VMEM((hp + tm, B_WIDTH), F32), pltpu.VMEM((dilp, B_WIDTH), F32)],
        compiler_params=_params(1),
    )(x, *consts, cprev, h0)


def _ab_out_kernel(x_ref, a_ref, y_ref, wa_ref, wy_ref, o_ref, *, attn_transposed):
    if attn_transposed:
        attn = lax.dot_general(a_ref[0], wa_ref[...], (((0,), (0,)), ((), ())),
                               preferred_element_type=F32)
    else:
        attn = _dot(a_ref[...], wa_ref[...])
    o_ref[...] = x_ref[...] + attn + _dot(y_ref[...], wy_ref[...])


def _ab_out_call(x, attn, yrg, w_attn, w_rg, geom):
    tm, tps = geom.tm, geom.tps
    tile = lambda n: pl.BlockSpec((tm, n), lambda i: (i, 0))
    attn_transposed = attn.ndim == 3
    if attn_transposed:
        a_spec = pl.BlockSpec((1, A_WIDTH, tm), lambda i: (i // tps, 0, i % tps))
    else:
        a_spec = tile(A_WIDTH)
    return pl.pallas_call(
        functools.partial(_ab_out_kernel, attn_transposed=attn_transposed),
        grid=(geom.n_tiles,),
        in_specs=[tile(D_MODEL), a_spec, tile(B_WIDTH),
                  _full_spec(w_attn.shape), _full_spec(w_rg.shape)],
        out_specs=tile(D_MODEL),
        out_shape=jax.ShapeDtypeStruct((geom.rows, D_MODEL), F32),
        compiler_params=_params(1),
    )(x, attn, yrg, w_attn, w_rg)


def _cumsum_kernel(x_ref, hi_ref, mid_ref, lo_ref, *, n):
    x = x_ref[...]
    lane = lax.broadcasted_iota(jnp.int32, x.shape, 1)
    s = 1
    while s < n:
        x = x + jnp.where(lane >= s, pltpu.roll(x, s, 1), 0.0)
        s *= 2
    x = x * LOG2E
    hi = x.astype(BF16)
    rest = x - hi.astype(F32)
    mid = rest.astype(BF16)
    hi_ref[...] = hi
    mid_ref[...] = mid
    lo_ref[...] = (rest - mid.astype(F32)).astype(BF16)


def _cumsum_call(x):
    out = jax.ShapeDtypeStruct(x.shape, BF16)
    return pl.pallas_call(
        functools.partial(_cumsum_kernel, n=x.shape[1]),
        grid=(1,),
        in_specs=[_full_spec(x.shape)],
        out_specs=[_full_spec(x.shape)] * 3,
        out_shape=[out] * 3,
        compiler_params=_params(1),
    )(x)


def _attn_prompt_kernel(qt_ref, cq_ref, kt_ref, ck_ref, vt_ref, o_ref, *, tq, sb, depth):
    qi = pl.program_id(2)
    qt = jnp.concatenate([qt_ref[0, 0], cq_ref[0, 0]], axis=0)
    n_sb = tq // sb
    row = lax.broadcasted_iota(jnp.int32, (sb, tq), 0)
    col = lax.broadcasted_iota(jnp.int32, (sb, tq), 1)

    def key_rows(j, s):
        return pl.ds(pl.multiple_of(j * tq + s * sb, sb), sb)

    def scores(j, s):
        kt = jnp.concatenate([kt_ref[0, 0, :, key_rows(j, s)], ck_ref[0, 0, :, key_rows(j, s)]], axis=0)
        return lax.dot_general(kt, qt, (((0,), (0,)), ((), ())), preferred_element_type=F32)

    def chunk(j, carry, diagonal):
        (m, l, acc), ahead = carry[0], list(carry[1])
        for s in range(n_sb):
            st = ahead.pop(0)
            if s + depth < n_sb:
                ahead.append(scores(j, s + depth))
            elif not diagonal:
                ahead.append(scores(j + 1, s + depth - n_sb))
            if diagonal:
                st = jnp.where(row + s * sb <= col, st, NEG)
            m_new = jnp.maximum(m, jnp.max(st, axis=0, keepdims=True))
            alpha = jnp.exp2(m - m_new)
            p = jnp.exp2(st - m_new)
            l = alpha * l + jnp.sum(p, axis=0, keepdims=True)
            acc = alpha * acc + _dot(vt_ref[0, 0, :, key_rows(j, s)], p.astype(BF16))
            m = m_new
        return (m, l, acc), tuple(ahead)

    state = (jnp.full((1, tq), NEG, F32), jnp.zeros((1, tq), F32), jnp.zeros((HEAD_DIM, tq), F32))
    carry = (state, tuple(scores(0, s) for s in range(depth)))
    carry = lax.fori_loop(0, qi, lambda j, c: chunk(j, c, False), carry)
    (_, l, acc), _ = chunk(qi, carry, True)
    o_ref[0, 0] = (acc / l).astype(BF16)


def _attn_prompt_call(q_t, cq, k_t, ck, v_t, tq):
    bsz, nh, dh, t = q_t.shape
    sb = min(128, tq)
    depth = min(2, tq // sb)
    tile = lambda r: pl.BlockSpec((1, 1, r, tq), lambda b, h, i: (b, h, 0, i))
    whole = lambda r: pl.BlockSpec((1, 1, r, t), lambda b, h, i: (b, h, 0, 0))
    return pl.pallas_call(
        functools.partial(_attn_prompt_kernel, tq=tq, sb=sb, depth=depth),
        grid=(bsz, nh, t // tq),
        in_specs=[tile(dh), tile(GATE_ROWS), whole(dh), whole(GATE_ROWS), whole(dh)],
        out_specs=tile(dh),
        out_shape=jax.ShapeDtypeStruct((bsz, nh, dh, t), BF16),
        compiler_params=_params(3),
    )(q_t, cq, k_t, ck, v_t)


def _attn_sample_kernel(pt_ref, q_ref, kn_ref, vn_ref, lfn_ref, *refs, n_pages, page, dt):
    del pt_ref
    k_refs = refs[:n_pages]
    v_refs = refs[n_pages:2 * n_pages]
    lf_refs = refs[2 * n_pages:3 * n_pages]
    o_ref = refs[3 * n_pages]
    nh = FOX_HEADS
    past = n_pages * page
    bdot = lambda a, b: lax.dot_general(a, b, (((2,), (1,)), ((0,), (0,))), preferred_element_type=F32)
    bdot_nt = lambda a, b: lax.dot_general(a, b, (((2,), (2,)), ((0,), (0,))), preferred_element_type=F32)

    q3 = q_ref[0]

    lft = jnp.concatenate([r[0, 0] for r in lf_refs], axis=1)
    lane = lax.broadcasted_iota(jnp.int32, lft.shape, 1)
    suf = lft
    s = 1
    while s < past:
        suf = suf + jnp.where(lane < past - s, pltpu.roll(suf, past - s, 1), 0.0)
        s *= 2
    suf = suf - lft
    suf3 = jnp.stack([jnp.broadcast_to(suf[h:h + 1, :], (dt, past)) for h in range(nh)], axis=0)

    lnew = jnp.broadcast_to(lfn_ref[0], (nh, dt, dt))
    colq = lax.broadcasted_iota(jnp.int32, (nh, dt, dt), 2)
    tq = lax.broadcasted_iota(jnp.int32, (nh, dt, dt), 1)
    causal = colq <= tq
    nq = jnp.sum(jnp.where(causal, lnew, 0.0), axis=2, keepdims=True)
    g = jnp.zeros((nh, dt, dt), F32)
    for l in range(dt):
        g = g + jnp.where(colq >= l, lnew[:, :, l:l + 1], 0.0)

    s_past = [bdot(q3, k_refs[p][0, 0].astype(BF16)) + (suf3[:, :, p * page:(p + 1) * page] + nq)
              for p in range(n_pages)]
    s_new = jnp.where(causal, bdot_nt(q3, kn_ref[0]) + (nq - g), NEG)

    m = jnp.max(s_new, axis=2, keepdims=True)
    for sp in s_past:
        m = jnp.maximum(m, jnp.max(sp, axis=2, keepdims=True))
    p_new = jnp.exp(s_new - m)
    l = jnp.sum(p_new, axis=2, keepdims=True)
    acc = bdot(p_new.astype(BF16), vn_ref[0])
    for p in range(n_pages):
        pp = jnp.exp(s_past[p] - m)
        l = l + jnp.sum(pp, axis=2, keepdims=True)
        acc = acc + bdot_nt(pp.astype(BF16), v_refs[p][0, 0].astype(BF16))
    o_ref[0] = (acc / l).astype(BF16)


def _attn_sample_call(page_table, q, k_new, v_new, lf_new, cache_kt, cache_vt, cache_lft, layer):
    db, nh, dt, dh = q.shape
    n_pages = page_table.shape[1]
    page = cache_kt.shape[-1]
    new_spec = pl.BlockSpec((1, nh, dt, dh), lambda b, pt: (b, 0, 0, 0))

    def page_spec(shape, p):
        zeros = (0,) * len(shape)
        return pl.BlockSpec((1, 1) + shape, lambda b, pt: (layer, pt[b, p]) + zeros)

    in_specs = ([new_spec, new_spec, new_spec,
                 pl.BlockSpec((1, nh, 1, dt), lambda b, pt: (b, 0, 0, 0))]
                + [page_spec((nh, dh, page), p) for p in range(n_pages)]
                + [page_spec((nh, dh, page), p) for p in range(n_pages)]
                + [page_spec((nh, page), p) for p in range(n_pages)])
    grid_spec = pltpu.PrefetchScalarGridSpec(
        num_scalar_prefetch=1, grid=(db,), in_specs=in_specs, out_specs=new_spec)
    return pl.pallas_call(
        functools.partial(_attn_sample_kernel, n_pages=n_pages, page=page, dt=dt),
        grid_spec=grid_spec,
        out_shape=jax.ShapeDtypeStruct((db, nh, dt, dh), BF16),
        compiler_params=_params(1),
    )(page_table, q, k_new, v_new, lf_new,
      *([cache_kt] * n_pages), *([cache_vt] * n_pages), *([cache_lft] * n_pages))


def _pool_kernel(xf_ref, xg_ref, gain_ref, prev_ref, w_ref, scale_ref, o_ref, st_ref,
                 inv_ref, ext_ref, *, tm, tps, dil, hp, pos0):
    i = pl.program_id(0)
    g = pl.program_id(1)
    first = (i % tps) == 0

    @pl.when(g == 0)
    def _():
        xf = xf_ref[...]
        inv_ref[...] = lax.rsqrt(jnp.mean(xf * xf, axis=-1, keepdims=True) + EPS)

    xg = xg_ref[...]
    xn = xg * inv_ref[...] * gain_ref[...]
    rows = lax.broadcasted_iota(jnp.int32, (tm, 1), 0) + (i % tps) * tm
    pos = lax.div(rows, jnp.int32(dil)) + pos0

    for gi, window in enumerate(POOL_WINDOWS):
        @pl.when(g == gi)
        def _(gi=gi, window=window):
            ext = ext_ref.at[gi]

            @pl.when(first)
            def _():
                ext[0:hp, :] = prev_ref[0]

            ext[hp:hp + tm, :] = xn
            total = xn
            for j in range(1, window):
                total = total + ext[hp - j * dil:hp - j * dil + tm, :]
            cnt = jnp.minimum(pos + 1, window).astype(F32)
            diff = total / cnt - xn
            y = _dot(diff.astype(BF16), w_ref[0]) * scale_ref[...]
            o_ref[...] = xg + y
            tail = ext[tm:tm + hp, :]
            ext[0:hp, :] = tail
            st_ref[0] = tail


def _pool_call(x, geom, gain, prev, w, scale, pos0):
    tm, hp = geom.tm, geom.halo(POOL_BUF + 1)
    ng = len(POOL_WINDOWS)
    kern = functools.partial(_pool_kernel, tm=tm, tps=geom.tps, dil=geom.dil, hp=hp, pos0=pos0)
    return pl.pallas_call(
        kern,
        grid=(geom.n_tiles, ng),
        in_specs=[pl.BlockSpec((tm, D_MODEL), lambda i, g: (i, 0)),
                  pl.BlockSpec((tm, POOL_GD), lambda i, g: (i, g)),
                  pl.BlockSpec((1, POOL_GD), lambda i, g: (0, g)),
                  pl.BlockSpec((1, hp, POOL_GD), lambda i, g: (i // geom.tps, 0, g)),
                  pl.BlockSpec((1, POOL_GD, POOL_GD), lambda i, g: (g, 0, 0)),
                  pl.BlockSpec((1, POOL_GD), lambda i, g: (0, g))],
        out_specs=[pl.BlockSpec((tm, POOL_GD), lambda i, g: (i, g)),
                   pl.BlockSpec((1, hp, POOL_GD), lambda i, g: (i, 0, g))],
        out_shape=[jax.ShapeDtypeStruct((geom.rows, D_MODEL), F32),
                   jax.ShapeDtypeStruct((geom.n_tiles, hp, D_MODEL), F32)],
        scratch_shapes=[pltpu.VMEM((tm, 1), F32), pltpu.VMEM((ng, hp + tm, POOL_GD), F32)],
        compiler_params=_params(2),
    )(x, x, gain, prev, w, scale)


def _ffn_kernel(x_ref, g_ref, wug_ref, wuv_ref, cwg_ref, cwv_ref, cbg_ref, cbv_ref, pg_ref, pv_ref,
                wd_ref, o_ref, sg_ref, sv_ref, xn_ref, carg_ref, carv_ref, *ext_refs,
                tm, tps, dil, hp, ck):
    i = pl.program_id(0)
    c = pl.program_id(1)
    first = (i % tps) == 0
    n_sub = len(ext_refs) // 2
    extg, extv = ext_refs[:n_sub], ext_refs[n_sub:]
    cols = lambda s: slice(s * ck, (s + 1) * ck)

    @pl.when(c == 0)
    def _():
        x = x_ref[...]
        xn_ref[...] = _rms(x, g_ref[...]).astype(BF16)
        o_ref[...] = x

    @pl.when(first)
    def _():
        for s in range(n_sub):
            extg[s][0:hp, :] = pg_ref[0, :, cols(s)]
            extv[s][0:hp, :] = pv_ref[0, :, cols(s)]

    @pl.when(jnp.logical_not(first))
    def _():
        for s in range(n_sub):
            extg[s][0:hp, :] = carg_ref[c, :, cols(s)]
            extv[s][0:hp, :] = carv_ref[c, :, cols(s)]

    xn = xn_ref[...]

    def up(s):
        extg[s][hp:hp + tm, :] = _dot(xn, wug_ref[:, cols(s)])
        extv[s][hp:hp + tm, :] = _dot(xn, wuv_ref[:, cols(s)])

    def gated(s):
        yg = _conv_from_ext(extg[s], cwg_ref[:, cols(s)], cbg_ref[:, cols(s)], FFN_CONV, dil, tm, hp)
        yv = _conv_from_ext(extv[s], cwv_ref[:, cols(s)], cbv_ref[:, cols(s)], FFN_CONV, dil, tm, hp)
        return (_gelu(yg) * yv).astype(BF16)

    ahead = min(2, n_sub)
    for s in range(ahead):
        up(s)
    acc = None
    for s in range(n_sub):
        if s + ahead < n_sub:
            up(s + ahead)
        down = _dot(gated(s), wd_ref[cols(s), :])
        acc = down if acc is None else acc + down
        for ext, car_ref, s_ref in ((extg[s], carg_ref, sg_ref), (extv[s], carv_ref, sv_ref)):
            tail = ext[tm:tm + hp, :]
            car_ref[c, :, cols(s)] = tail
            s_ref[0, :, cols(s)] = tail
    o_ref[...] += acc


def _ffn_call(x, geom, gain, w_up, conv_w, conv_b, prev, w_down, block, ck):
    tm, hp = geom.tm, geom.halo(FFN_CONV)
    nblk = D_FF // block
    n_sub = block // ck
    tps = geom.tps
    kern = functools.partial(_ffn_kernel, tm=tm, tps=tps, dil=geom.dil, hp=hp, ck=ck)
    resident = {"pipeline_mode": pl.Buffered(1)} if nblk == 1 else {}
    col = lambda r, off, **kw: pl.BlockSpec((r, block), lambda i, c: (0, off + c), **kw)
    st_in = lambda off: pl.BlockSpec((1, hp, block), lambda i, c: (i // tps, 0, off + c))
    st_out = pl.BlockSpec((1, hp, block), lambda i, c: (i, 0, c))
    return pl.pallas_call(
        kern,
        grid=(geom.n_tiles, nblk),
        in_specs=[pl.BlockSpec((tm, D_MODEL), lambda i, c: (i, 0)),
                  _full_spec(gain.shape),
                  col(D_MODEL, 0, **resident), col(D_MODEL, nblk, **resident),
                  col(FFN_CONV, 0), col(FFN_CONV, nblk),
                  col(1, 0), col(1, nblk),
                  st_in(0), st_in(nblk),
                  pl.BlockSpec((block, D_MODEL), lambda i, c: (c, 0), **resident)],
        out_specs=[pl.BlockSpec((tm, D_MODEL), lambda i, c: (i, 0)), st_out, st_out],
        out_shape=[jax.ShapeDtypeStruct((geom.rows, D_MODEL), F32),
                   jax.ShapeDtypeStruct((geom.n_tiles, hp, D_FF), F32),
                   jax.ShapeDtypeStruct((geom.n_tiles, hp, D_FF), F32)],
        scratch_shapes=[pltpu.VMEM((tm, D_MODEL), BF16),
                        pltpu.VMEM((nblk, hp, block), F32), pltpu.VMEM((nblk, hp, block), F32)]
        + [pltpu.VMEM((hp + tm, ck), F32)] * (2 * n_sub),
        compiler_params=_params(2),
    )(x, gain, w_up, w_up, conv_w, conv_w, conv_b, conv_b, prev, prev, w_down)


def _state_to_halo(state, geom, hp):
    n, w1, ch = state.shape
    if geom.dil == 1:
        rows = state
    else:
        rows = state.transpose(1, 0, 2).reshape(1, w1 * n, ch)
    return jnp.pad(rows, ((0, 0), (hp - rows.shape[1], 0), (0, 0)))


def _halo_to_state(halo, geom, n, w1):
    ch = halo.shape[-1]
    if halo.shape[0] != geom.n_seq:
        halo = halo[geom.tps - 1::geom.tps]
    if geom.dil == 1:
        return halo[:, halo.shape[1] - w1:, :]
    return halo[0, halo.shape[1] - w1 * n:, :].reshape(w1, n, ch).transpose(1, 0, 2)


def _block_diag(w):
    nb, bi, bj = w.shape
    eye = jnp.eye(nb, dtype=w.dtype)
    return (eye[:, None, :, None] * w[:, :, None, :]).reshape(nb * bi, nb * bj)


def _trunk(x_rows, geom, n, pos0, p, paged, rg_h, rg_conv, pool_buf, ffn_buf, ffn_ck):
    depth = p["norm_mix"].shape[0]
    ks, vs, lfs, hs, cs, pbs, fbs = [], [], [], [], [], [], []
    dil = geom.dil
    t_steps = geom.rows // n
    dilp = _round_up(dil, SUBLANES)
    x = x_rows
    for layer in range(depth):
        li = layer // 2
        if layer % 2 == 0:
            w_in = p["ab_w_in"][li]
            a3 = 3 * A_WIDTH
            w = {
                "g": p["norm_mix"][layer][None, :],
                "wqk": w_in[:, :2 * A_WIDTH].astype(BF16),
                "wv": w_in[:, 2 * A_WIDTH:a3].astype(BF16),
                "wf": jnp.pad(w_in[:, a3:a3 + FOX_HEADS], ((0, 0), (0, F_PAD - FOX_HEADS))).astype(BF16),
                "wrg": w_in[:, a3 + FOX_HEADS:].astype(BF16),
                "bf": jnp.pad(p["ab_b_f"][li], (0, F_PAD - FOX_HEADS))[None, :],
                "qg": jnp.tile(p["ab_q_gain"][li], FOX_HEADS)[None, :],
                "kg": jnp.tile(p["ab_k_gain"][li], FOX_HEADS)[None, :],
                "bd": _block_diag(jnp.full((FOX_HEADS, HEAD_DIM, HEAD_DIM), 1.0 / HEAD_DIM, F32)).astype(BF16),
                "cw": p["ab_conv_w"][li], "cb": p["ab_conv_b"][li][None, :],
                "wa": _block_diag(p["ab_w_a"][li]).astype(BF16), "ba": p["ab_b_a"][li][None, :],
                "wx": _block_diag(p["ab_w_x"][li]).astype(BF16), "bx": p["ab_b_x"][li][None, :],
                "lam": p["ab_lambda"][li][None, :],
            }
            hp = geom.halo(RG_CONV)
            cprev = _state_to_halo(rg_conv[li], geom, hp)
            if dil == 1:
                h0 = jnp.pad(rg_h[li][:, None, :], ((0, 0), (0, dilp - 1), (0, 0)))
            else:
                h0 = jnp.pad(rg_h[li][None], ((0, 0), (0, dilp - dil), (0, 0)))
            scale = HEAD_DIM ** -0.5
            q_mult = scale * LOG2E if paged is None else scale
            q, k, v, kt, vt, lft, yrg, cst, hl = _ab_in_call(x, geom, w, cprev, h0, q_mult)

            if paged is None:
                parts = _cumsum_call(lft.reshape(n * FOX_HEADS, t_steps))
                c3 = jnp.stack(parts, axis=1).reshape(n, FOX_HEADS, 3, t_steps)
                ones = jnp.ones_like(c3)
                fill = jnp.zeros((n, FOX_HEADS, GATE_ROWS - 6, t_steps), BF16)
                cq = jnp.concatenate([-ones, c3, fill], axis=2)
                ck = jnp.concatenate([c3, ones, fill], axis=2)
                heads_t = lambda z: z.reshape(n, FOX_HEADS, HEAD_DIM, t_steps)
                attn = _attn_prompt_call(heads_t(q), cq, heads_t(k), ck, heads_t(v), min(512, t_steps))
                attn = attn.reshape(n, A_WIDTH, t_steps)
                ks.append(kt.reshape(n, FOX_HEADS, HEAD_DIM, t_steps).transpose(0, 3, 1, 2))
                vs.append(vt.reshape(n, FOX_HEADS, HEAD_DIM, t_steps).transpose(0, 3, 1, 2))
                lfs.append(lft.transpose(0, 2, 1))
                hs.append(hl[:, 0, :])
            else:
                ckt, cvt, clft, page_table = paged

                def heads_bm(z):
                    return z.reshape(t_steps, n, FOX_HEADS, HEAD_DIM).transpose(1, 2, 0, 3)
                lf_new = lft.transpose(2, 1, 0)[:, :, None, :]
                attn = _attn_sample_call(page_table, heads_bm(q), heads_bm(k), heads_bm(v), lf_new,
                                         ckt, cvt, clft, li)
                attn = attn.transpose(2, 0, 1, 3).reshape(geom.rows, A_WIDTH)
                ks.append(kt.reshape(t_steps, FOX_HEADS, HEAD_DIM, n).transpose(3, 0, 1, 2))
                vs.append(vt.reshape(t_steps, FOX_HEADS, HEAD_DIM, n).transpose(3, 0, 1, 2))
                lfs.append(lft.transpose(2, 0, 1))
                hs.append(hl[0, :dil, :])
            cs.append(_halo_to_state(cst, geom, n, RG_CONV - 1))
            w_out = p["ab_w_out"][li].astype(BF16)
            x = _ab_out_call(x, attn, yrg, w_out[:A_WIDTH], w_out[A_WIDTH:], geom)
        else:
            hp = geom.halo(POOL_BUF + 1)
            prev = _state_to_halo(pool_buf[li], geom, hp)
            x, st = _pool_call(x, geom, p["norm_mix"][layer][None, :], prev,
                               p["pool_w"][li].astype(BF16), p["pool_scale"][li][None, :], pos0)
            pbs.append(_halo_to_state(st, geom, n, POOL_BUF))
        hp = geom.halo(FFN_CONV)
        prev = _state_to_halo(ffn_buf[layer], geom, hp)
        x, sg, sv = _ffn_call(x, geom, p["norm_ffn"][layer][None, :], p["ffn_w_up"][layer].astype(BF16),
                              p["ffn_conv_w"][layer], p["ffn_conv_b"][layer][None, :], prev,
                              p["ffn_w_down"][layer].astype(BF16), *ffn_ck)
        fbs.append(_halo_to_state(jnp.concatenate([sg, sv], axis=-1), geom, n, FFN_CONV - 1))
    return x, (jnp.stack(ks), jnp.stack(vs), jnp.stack(lfs), jnp.stack(hs), jnp.stack(cs),
               jnp.stack(pbs), jnp.stack(fbs))


def kernel(x_prompt, x_sample, cache_k, cache_v, cache_logf, state_rg_h, state_rg_conv, state_pool, state_ffn_conv, page_table, norm_mix, norm_ffn, ab_w_in, ab_b_f, ab_q_gain, ab_k_gain, ab_conv_w, ab_conv_b, ab_w_a, ab_b_a, ab_w_x, ab_b_x, ab_lambda, ab_w_out, pool_w, pool_scale, ffn_w_up, ffn_conv_w, ffn_conv_b, ffn_w_down):
    p = {
        "norm_mix": norm_mix, "norm_ffn": norm_ffn,
        "ab_w_in": ab_w_in, "ab_b_f": ab_b_f, "ab_q_gain": ab_q_gain, "ab_k_gain": ab_k_gain,
        "ab_conv_w": ab_conv_w, "ab_conv_b": ab_conv_b, "ab_w_a": ab_w_a, "ab_b_a": ab_b_a,
        "ab_w_x": ab_w_x, "ab_b_x": ab_b_x, "ab_lambda": ab_lambda, "ab_w_out": ab_w_out,
        "pool_w": pool_w, "pool_scale": pool_scale,
        "ffn_w_up": ffn_w_up, "ffn_conv_w": ffn_conv_w, "ffn_conv_b": ffn_conv_b, "ffn_w_down": ffn_w_down,
    }
    depth = norm_mix.shape[0]
    n_ab, n_pool = (depth + 1) // 2, depth // 2

    bsz, t, _ = x_prompt.shape
    geom_p = _Geom(bsz, t, 1, 512)
    y_p, st_p = _trunk(
        x_prompt.reshape(bsz * t, D_MODEL), geom_p, bsz, 0, p, None,
        jnp.zeros((n_ab, bsz, B_WIDTH), F32), jnp.zeros((n_ab, bsz, RG_CONV - 1, B_WIDTH), F32),
        jnp.zeros((n_pool, bsz, POOL_BUF, D_MODEL), F32),
        jnp.zeros((depth, bsz, FFN_CONV - 1, 2 * D_FF), F32), (D_FF, 1024))
    y_prompt = y_p.reshape(bsz, t, D_MODEL)

    db, dt, _ = x_sample.shape
    n_pool_pages, page = cache_k.shape[1], cache_k.shape[2]
    past_len = page_table.shape[1] * page
    geom_s = _Geom(1, dt * db, db, 512)
    paged = (cache_k.transpose(0, 1, 3, 4, 2), cache_v.transpose(0, 1, 3, 4, 2),
             cache_logf.transpose(0, 1, 3, 2), page_table)
    y_s, st_s = _trunk(
        x_sample.transpose(1, 0, 2).reshape(dt * db, D_MODEL), geom_s, db, past_len, p, paged,
        state_rg_h, state_rg_conv, state_pool, state_ffn_conv, (512, 512))
    y_sample = y_s.reshape(dt, db, D_MODEL).transpose(1, 0, 2)
    return (y_prompt, y_sample) + st_p + st_s
```

```python
import functools

import jax
import jax.numpy as jnp
from jax import lax
from jax.experimental import pallas as pl
from jax.experimental.pallas import tpu as pltpu

D_MODEL = 1024
A_WIDTH = 512
B_WIDTH = 512
HEAD_DIM = 64
FOX_HEADS = 8
RG_CONV = 4
RG_C = 8.0
POOL_WINDOWS = (2, 4, 8, 16)
POOL_GD = 256
POOL_BUF = 15
D_FF = 3072
FFN_CONV = 3
EPS = 1e-6
NEG = -1e30
F_PAD = 128
GATE_ROWS = 16
ATTN_HEADS_PER_STEP = 2
LOG2E = 1.4426950408889634
SUBLANES = 8
F32 = jnp.float32
BF16 = jnp.bfloat16
VMEM_LIMIT_BYTES = 56 * 1024 * 1024


def _round_up(x, m):
    return -(-x // m) * m


def _full_spec(shape):
    return pl.BlockSpec(shape, lambda *_: (0,) * len(shape))


def _params(n_axes):
    return pltpu.CompilerParams(dimension_semantics=("arbitrary",) * n_axes,
                                vmem_limit_bytes=VMEM_LIMIT_BYTES)


def _dot(a, b):
    return jnp.dot(a, b, preferred_element_type=F32)


def _dot_nt(a, b):
    return lax.dot_general(a, b, (((1,), (1,)), ((), ())), preferred_element_type=F32)


def _rms(x, gain):
    y = x * lax.rsqrt(jnp.mean(x * x, axis=-1, keepdims=True) + EPS)
    return y * gain


def _gelu(x):
    return 0.5 * x * (1.0 + jnp.tanh(0.7978845608028654 * (x + 0.044715 * (x * x * x))))


def _softplus(x):
    return jnp.maximum(x, 0.0) + jnp.log1p(jnp.exp(-jnp.abs(x)))


class _Geom:
    def __init__(self, n_seq, rows_per_seq, dil, tile):
        self.n_seq, self.rows_per_seq, self.dil = n_seq, rows_per_seq, dil
        self.tm = min(tile, rows_per_seq)
        assert rows_per_seq % self.tm == 0 and self.tm % dil == 0 and self.tm % SUBLANES == 0
        self.tps = rows_per_seq // self.tm
        self.rows = n_seq * rows_per_seq
        self.n_tiles = self.rows // self.tm

    def halo(self, width):
        return _round_up((width - 1) * self.dil, SUBLANES)


def _conv_from_ext(ext_ref, w_ref, bias, width, dil, tm, hp):
    y = bias
    for j in range(width):
        off = hp - (width - 1 - j) * dil
        y = y + ext_ref[off:off + tm, :] * w_ref[j:j + 1, :]
    return y


def _ab_in_kernel(x_ref, g_ref, wqk_ref, wv_ref, wrg_ref, wf_ref, bf_ref, qg_ref, kg_ref, bd_ref,
                  cw_ref, cb_ref, wa_ref, ba_ref, wx_ref, bx_ref, lam_ref, cprev_ref, h0_ref,
                  q_out, k_out, v_out, kt_out, vt_out, lft_out, y_out, cst_out, h_out,
                  ext_ref, hc_ref, *, tm, tps, dil, hp, q_mult):
    i = pl.program_id(0)
    first = (i % tps) == 0
    xn = _rms(x_ref[...], g_ref[...]).astype(BF16)

    bd = bd_ref[...]

    def head_norm(z, gain):
        z2 = z * z
        hi = z2.astype(BF16)
        lo = (z2 - hi.astype(F32)).astype(BF16)
        ms = _dot(hi, bd) + _dot(lo, bd)
        return z * lax.rsqrt(ms + EPS) * gain

    qk = _dot(xn, wqk_ref[...])
    q = head_norm(qk[:, :A_WIDTH], qg_ref[...])
    k = head_norm(qk[:, A_WIDTH:], kg_ref[...])
    v = _dot(xn, wv_ref[...])
    q = q * q_mult

    f = _dot(xn, wf_ref[...]) + bf_ref[...]
    lf = jnp.minimum(f, 0.0) - jnp.log1p(jnp.exp(-jnp.abs(f)))

    unit = tm if dil == 1 else dil
    for u in range(tm // unit):
        rows_u = slice(u * unit, (u + 1) * unit)
        kt = k[rows_u, :].T
        vt = v[rows_u, :].T
        kt_out[u] = kt
        vt_out[u] = vt
        lft_out[u] = lf[rows_u, :].T[:FOX_HEADS, :]
        if dil == 1:
            q_out[u] = q.T.astype(BF16)
            k_out[u] = kt.astype(BF16)
            v_out[u] = vt.astype(BF16)
    if dil != 1:
        q_out[...] = q.astype(BF16)
        k_out[...] = k.astype(BF16)
        v_out[...] = v.astype(BF16)

    rg = _dot(xn, wrg_ref[...])
    xr = rg[:, :B_WIDTH]
    gate = rg[:, B_WIDTH:]

    @pl.when(first)
    def _():
        ext_ref[0:hp, :] = cprev_ref[0]
        hc_ref[...] = h0_ref[0]

    ext_ref[hp:hp + tm, :] = xr
    xc = _conv_from_ext(ext_ref, cw_ref, cb_ref[...], RG_CONV, dil, tm, hp)
    tail = ext_ref[tm:tm + hp, :]
    ext_ref[0:hp, :] = tail
    cst_out[0] = tail

    xcb = xc.astype(BF16)
    r = jax.nn.sigmoid(_dot(xcb, wa_ref[...]) + ba_ref[...])
    gi = jax.nn.sigmoid(_dot(xcb, wx_ref[...]) + bx_ref[...])
    log_a = -RG_C * r * _softplus(-lam_ref[...])
    a = jnp.exp(log_a)
    inp = jnp.sqrt(-jnp.tanh(log_a) * (a * a + 1.0)) * gi * xc

    rows = lax.broadcasted_iota(jnp.int32, (tm, 1), 0)
    s = dil
    while s < tm:
        valid = rows >= s
        a_sh = jnp.where(valid, pltpu.roll(a, s, 0), 1.0)
        h_sh = jnp.where(valid, pltpu.roll(inp, s, 0), 0.0)
        inp = a * h_sh + inp
        a = a * a_sh
        s *= 2

    hc = hc_ref[0:dil, :]
    if dil == 1:
        hc_rows = jnp.broadcast_to(hc, (tm, B_WIDTH))
    else:
        hc_rows = jnp.concatenate([hc] * (tm // dil), axis=0)
    h = inp + a * hc_rows
    hc_ref[0:dil, :] = h[tm - dil:tm, :]
    h_out[0] = hc_ref[...]
    y_out[...] = (h * _gelu(gate)).astype(BF16)


def _ab_in_call(x, geom, w, cprev, h0, q_mult):
    tm, hp, tps = geom.tm, geom.halo(RG_CONV), geom.tps
    dilp = _round_up(geom.dil, SUBLANES)
    rows = geom.rows
    tile = lambda n: pl.BlockSpec((tm, n), lambda i: (i, 0))
    seq3 = lambda r, n: pl.BlockSpec((1, r, n), lambda i: (i // tps, 0, 0))
    if geom.dil == 1:
        t_shape = lambda ch: (geom.n_seq, ch, geom.rows_per_seq)
        t_spec = lambda ch: pl.BlockSpec((1, ch, tm), lambda i: (i // tps, 0, i % tps))
        op_spec, op_shape = t_spec(A_WIDTH), jax.ShapeDtypeStruct(t_shape(A_WIDTH), BF16)
    else:
        t_shape = lambda ch: (rows // geom.dil, ch, geom.dil)
        t_spec = lambda ch: pl.BlockSpec((tm // geom.dil, ch, geom.dil), lambda i: (i, 0, 0))
        op_spec, op_shape = tile(A_WIDTH), jax.ShapeDtypeStruct((rows, A_WIDTH), BF16)
    consts = [w["g"], w["wqk"], w["wv"], w["wrg"], w["wf"], w["bf"], w["qg"], w["kg"], w["bd"],
              w["cw"], w["cb"], w["wa"], w["ba"], w["wx"], w["bx"], w["lam"]]
    kern = functools.partial(_ab_in_kernel, tm=tm, tps=tps, dil=geom.dil, hp=hp, q_mult=q_mult)
    return pl.pallas_call(
        kern,
        grid=(geom.n_tiles,),
        in_specs=[tile(D_MODEL)] + [_full_spec(c.shape) for c in consts]
        + [seq3(hp, B_WIDTH), seq3(dilp, B_WIDTH)],
        out_specs=[op_spec, op_spec, op_spec,
                   t_spec(A_WIDTH), t_spec(A_WIDTH), t_spec(FOX_HEADS), tile(B_WIDTH),
                   seq3(hp, B_WIDTH), seq3(dilp, B_WIDTH)],
        out_shape=[op_shape, op_shape, op_shape,
                   jax.ShapeDtypeStruct(t_shape(A_WIDTH), F32),
                   jax.ShapeDtypeStruct(t_shape(A_WIDTH), F32),
                   jax.ShapeDtypeStruct(t_shape(FOX_HEADS), F32),
                   jax.ShapeDtypeStruct((rows, B_WIDTH), BF16),
                   jax.ShapeDtypeStruct((geom.n_seq, hp, B_WIDTH), F32),
                   jax.ShapeDtypeStruct((geom.n_seq, dilp, B_WIDTH), F32)],
        scratch_shapes=[pltpu.VMEM((hp + tm, B_WIDTH), F32), pltpu.VMEM((dilp, B_WIDTH), F32)],
        compiler_params=_params(1),
    )(x, *consts, cprev, h0)


def _ab_out_kernel(x_ref, a_ref, y_ref, wa_ref, wy_ref, o_ref, *, attn_transposed):
    if attn_transposed:
        attn = lax.dot_general(a_ref[0], wa_ref[...], (((0,), (0,)), ((), ())),
                               preferred_element_type=F32)
    else:
        attn = _dot(a_ref[...], wa_ref[...])
    o_ref[...] = x_ref[...] + attn + _dot(y_ref[...], wy_ref[...])


def _ab_out_call(x, attn, yrg, w_attn, w_rg, geom):
    tm, tps = geom.tm, geom.tps
    tile = lambda n: pl.BlockSpec((tm, n), lambda i: (i, 0))
    attn_transposed = attn.ndim == 3
    if attn_transposed:
        a_spec = pl.BlockSpec((1, A_WIDTH, tm), lambda i: (i // tps, 0, i % tps))
    else:
        a_spec = tile(A_WIDTH)
    return pl.pallas_call(
        functools.partial(_ab_out_kernel, attn_transposed=attn_transposed),
        grid=(geom.n_tiles,),
        in_specs=[tile(D_MODEL), a_spec, tile(B_WIDTH),
                  _full_spec(w_attn.shape), _full_spec(w_rg.shape)],
        out_specs=tile(D_MODEL),
        out_shape=jax.ShapeDtypeStruct((geom.rows, D_MODEL), F32),
        compiler_params=_params(1),
    )(x, attn, yrg, w_attn, w_rg)


def _cumsum_kernel(x_ref, cq_ref, ck_ref, *, n):
    x = x_ref[...]
    lane = lax.broadcasted_iota(jnp.int32, x.shape, 1)
    s = 1
    while s < n:
        x = x + jnp.where(lane >= s, pltpu.roll(x, s, 1), 0.0)
        s *= 2
    x = x * LOG2E
    hi = x.astype(BF16).astype(F32)
    rest = x - hi
    mid = rest.astype(BF16).astype(F32)
    lo = rest - mid
    row = lax.broadcasted_iota(jnp.int32, (GATE_ROWS, n), 0)
    for r in range(x.shape[0]):
        parts = [jnp.broadcast_to(p[r:r + 1, :], (GATE_ROWS, n)) for p in (hi, mid, lo)]
        cq = jnp.where(row < 3, -1.0, 0.0)
        ck = jnp.where((row >= 3) & (row < 6), 1.0, 0.0)
        for j, part in enumerate(parts):
            cq = jnp.where(row == 3 + j, part, cq)
            ck = jnp.where(row == j, part, ck)
        cq_ref[r] = cq.astype(BF16)
        ck_ref[r] = ck.astype(BF16)


def _cumsum_call(x):
    rows, n = x.shape
    out = jax.ShapeDtypeStruct((rows, GATE_ROWS, n), BF16)
    return pl.pallas_call(
        functools.partial(_cumsum_kernel, n=n),
        grid=(1,),
        in_specs=[_full_spec(x.shape)],
        out_specs=[_full_spec(out.shape)] * 2,
        out_shape=[out] * 2,
        compiler_params=_params(1),
    )(x)


def _attn_prompt_kernel(qt_ref, cq_ref, kt_ref, ck_ref, vt_ref, o_ref, *, tq, sb, depth):
    qi = pl.program_id(2)
    n_heads = qt_ref.shape[1]
    heads = range(n_heads)
    qt = [jnp.concatenate([qt_ref[0, h], cq_ref[0, h]], axis=0) for h in heads]
    n_sb = tq // sb
    row = lax.broadcasted_iota(jnp.int32, (sb, tq), 0)
    col = lax.broadcasted_iota(jnp.int32, (sb, tq), 1)

    def key_rows(j, s):
        return pl.ds(pl.multiple_of(j * tq + s * sb, sb), sb)

    def scores(h, j, s):
        kt = jnp.concatenate([kt_ref[0, h, :, key_rows(j, s)], ck_ref[0, h, :, key_rows(j, s)]], axis=0)
        return lax.dot_general(kt, qt[h], (((0,), (0,)), ((), ())), preferred_element_type=F32)

    def chunk(j, carry, diagonal):
        state, ahead = list(carry[0]), [list(a) for a in carry[1]]
        for s in range(n_sb):
            for h in heads:
                m, l, acc = state[h]
                st = ahead[h].pop(0)
                if s + depth < n_sb:
                    ahead[h].append(scores(h, j, s + depth))
                elif not diagonal:
                    ahead[h].append(scores(h, j + 1, s + depth - n_sb))
                if diagonal:
                    st = jnp.where(row + s * sb <= col, st, NEG)
                m_new = jnp.maximum(m, jnp.max(st, axis=0, keepdims=True))
                alpha = jnp.exp2(m - m_new)
                p = jnp.exp2(st - m_new)
                l = alpha * l + jnp.sum(p, axis=0, keepdims=True)
                acc = alpha * acc + _dot(vt_ref[0, h, :, key_rows(j, s)], p.astype(BF16))
                state[h] = (m_new, l, acc)
        return tuple(state), tuple(tuple(a) for a in ahead)

    state = tuple((jnp.full((1, tq), NEG, F32), jnp.zeros((1, tq), F32), jnp.zeros((HEAD_DIM, tq), F32))
                  for _ in heads)
    carry = (state, tuple(tuple(scores(h, 0, s) for s in range(depth)) for h in heads))
    carry = lax.fori_loop(0, qi, lambda j, c: chunk(j, c, False), carry)
    state, _ = chunk(qi, carry, True)
    for h, (_, l, acc) in enumerate(state):
        o_ref[0, h] = (acc / l).astype(BF16)


def _attn_prompt_call(q_t, cq, k_t, ck, v_t, tq):
    bsz, nh, dh, t = q_t.shape
    sb = min(128, tq)
    depth = min(2, tq // sb)
    hg = ATTN_HEADS_PER_STEP
    tile = lambda r: pl.BlockSpec((1, hg, r, tq), lambda b, h, i: (b, h, 0, i))
    whole = lambda r: pl.BlockSpec((1, hg, r, t), lambda b, h, i: (b, h, 0, 0))
    return pl.pallas_call(
        functools.partial(_attn_prompt_kernel, tq=tq, sb=sb, depth=depth),
        grid=(bsz, nh // hg, t // tq),
        in_specs=[tile(dh), tile(GATE_ROWS), whole(dh), whole(GATE_ROWS), whole(dh)],
        out_specs=tile(dh),
        out_shape=jax.ShapeDtypeStruct((bsz, nh, dh, t), BF16),
        compiler_params=_params(3),
    )(q_t, cq, k_t, ck, v_t)


def _attn_sample_kernel(pt_ref, q_ref, kn_ref, vn_ref, lfn_ref, *refs, n_pages, page, dt):
    del pt_ref
    k_refs = refs[:n_pages]
    v_refs = refs[n_pages:2 * n_pages]
    lf_refs = refs[2 * n_pages:3 * n_pages]
    o_ref = refs[3 * n_pages]
    nh = FOX_HEADS
    past = n_pages * page
    bdot = lambda a, b: lax.dot_general(a, b, (((2,), (1,)), ((0,), (0,))), preferred_element_type=F32)
    bdot_nt = lambda a, b: lax.dot_general(a, b, (((2,), (2,)), ((0,), (0,))), preferred_element_type=F32)

    q3 = q_ref[0]

    lft = jnp.concatenate([r[0, 0] for r in lf_refs], axis=1)
    lane = lax.broadcasted_iota(jnp.int32, lft.shape, 1)
    suf = lft
    s = 1
    while s < past:
        suf = suf + jnp.where(lane < past - s, pltpu.roll(suf, past - s, 1), 0.0)
        s *= 2
    suf = suf - lft
    suf3 = jnp.stack([jnp.broadcast_to(suf[h:h + 1, :], (dt, past)) for h in range(nh)], axis=0)

    lnew = jnp.broadcast_to(lfn_ref[0], (nh, dt, dt))
    colq = lax.broadcasted_iota(jnp.int32, (nh, dt, dt), 2)
    tq = lax.broadcasted_iota(jnp.int32, (nh, dt, dt), 1)
    causal = colq <= tq
    nq = jnp.sum(jnp.where(causal, lnew, 0.0), axis=2, keepdims=True)
    g = jnp.zeros((nh, dt, dt), F32)
    for l in range(dt):
        g = g + jnp.where(colq >= l, lnew[:, :, l:l + 1], 0.0)

    s_past = [bdot(q3, k_refs[p][0, 0].astype(BF16)) + (suf3[:, :, p * page:(p + 1) * page] + nq)
              for p in range(n_pages)]
    s_new = jnp.where(causal, bdot_nt(q3, kn_ref[0]) + (nq - g), NEG)

    m = jnp.max(s_new, axis=2, keepdims=True)
    for sp in s_past:
        m = jnp.maximum(m, jnp.max(sp, axis=2, keepdims=True))
    p_new = jnp.exp(s_new - m)
    l = jnp.sum(p_new, axis=2, keepdims=True)
    acc = bdot(p_new.astype(BF16), vn_ref[0])
    for p in range(n_pages):
        pp = jnp.exp(s_past[p] - m)
        l = l + jnp.sum(pp, axis=2, keepdims=True)
        acc = acc + bdot_nt(pp.astype(BF16), v_refs[p][0, 0].astype(BF16))
    o_ref[0] = (acc / l).astype(BF16)


def _attn_sample_call(page_table, q, k_new, v_new, lf_new, cache_kt, cache_vt, cache_lft, layer):
    db, nh, dt, dh = q.shape
    n_pages = page_table.shape[1]
    page = cache_kt.shape[-1]
    new_spec = pl.BlockSpec((1, nh, dt, dh), lambda b, pt: (b, 0, 0, 0))

    def page_spec(shape, p):
        zeros = (0,) * len(shape)
        return pl.BlockSpec((1, 1) + shape, lambda b, pt: (layer, pt[b, p]) + zeros)

    in_specs = ([new_spec, new_spec, new_spec,
                 pl.BlockSpec((1, nh, 1, dt), lambda b, pt: (b, 0, 0, 0))]
                + [page_spec((nh, dh, page), p) for p in range(n_pages)]
                + [page_spec((nh, dh, page), p) for p in range(n_pages)]
                + [page_spec((nh, page), p) for p in range(n_pages)])
    grid_spec = pltpu.PrefetchScalarGridSpec(
        num_scalar_prefetch=1, grid=(db,), in_specs=in_specs, out_specs=new_spec)
    return pl.pallas_call(
        functools.partial(_attn_sample_kernel, n_pages=n_pages, page=page, dt=dt),
        grid_spec=grid_spec,
        out_shape=jax.ShapeDtypeStruct((db, nh, dt, dh), BF16),
        compiler_params=_params(1),
    )(page_table, q, k_new, v_new, lf_new,
      *([cache_kt] * n_pages), *([cache_vt] * n_pages), *([cache_lft] * n_pages))


def _pool_kernel(x_ref, gain_ref, prev_ref, w_ref, scale_ref, o_ref, st_ref, ext_ref,
                 *, tm, tps, dil, hp, pos0):
    i = pl.program_id(0)
    first = (i % tps) == 0
    x = x_ref[...]
    xn = _rms(x, gain_ref[...])
    rows = lax.broadcasted_iota(jnp.int32, (tm, 1), 0) + (i % tps) * tm
    pos = lax.div(rows, jnp.int32(dil)) + pos0

    @pl.when(first)
    def _():
        ext_ref[0:hp, :] = prev_ref[0]

    ext_ref[hp:hp + tm, :] = xn
    for gi, window in enumerate(POOL_WINDOWS):
        cols = slice(gi * POOL_GD, (gi + 1) * POOL_GD)
        xg = xn[:, cols]
        total = xg
        for j in range(1, window):
            total = total + ext_ref[hp - j * dil:hp - j * dil + tm, cols]
        cnt = jnp.minimum(pos + 1, window).astype(F32)
        diff = total / cnt - xg
        y = _dot(diff.astype(BF16), w_ref[gi]) * scale_ref[:, cols]
        o_ref[:, cols] = x[:, cols] + y
    tail = ext_ref[tm:tm + hp, :]
    ext_ref[0:hp, :] = tail
    st_ref[0] = tail


def _pool_call(x, geom, gain, prev, w, scale, pos0):
    tm, hp = geom.tm, geom.halo(POOL_BUF + 1)
    kern = functools.partial(_pool_kernel, tm=tm, tps=geom.tps, dil=geom.dil, hp=hp, pos0=pos0)
    resident = {"pipeline_mode": pl.Buffered(1)} if geom.n_seq == 1 else {}
    return pl.pallas_call(
        kern,
        grid=(geom.n_tiles,),
        in_specs=[pl.BlockSpec((tm, D_MODEL), lambda i: (i, 0)),
                  _full_spec(gain.shape),
                  pl.BlockSpec((1, hp, D_MODEL), lambda i: (i // geom.tps, 0, 0), **resident),
                  _full_spec(w.shape),
                  _full_spec(scale.shape)],
        out_specs=[pl.BlockSpec((tm, D_MODEL), lambda i: (i, 0)),
                   pl.BlockSpec((1, hp, D_MODEL), lambda i: (i, 0, 0))],
        out_shape=[jax.ShapeDtypeStruct((geom.rows, D_MODEL), F32),
                   jax.ShapeDtypeStruct((geom.n_tiles, hp, D_MODEL), F32)],
        scratch_shapes=[pltpu.VMEM((hp + tm, D_MODEL), F32)],
        compiler_params=_params(1),
    )(x, gain, prev, w, scale)


def _ffn_kernel(x_ref, g_ref, wug_ref, wuv_ref, cwg_ref, cwv_ref, cbg_ref, cbv_ref, pg_ref, pv_ref,
                wd_ref, o_ref, sg_ref, sv_ref, xn_ref, carg_ref, carv_ref, *ext_refs,
                tm, tps, dil, hp, ck):
    i = pl.program_id(0)
    c = pl.program_id(1)
    first = (i % tps) == 0
    n_sub = len(ext_refs) // 2
    extg, extv = ext_refs[:n_sub], ext_refs[n_sub:]
    cols = lambda s: slice(s * ck, (s + 1) * ck)

    @pl.when(c == 0)
    def _():
        x = x_ref[...]
        xn_ref[...] = _rms(x, g_ref[...]).astype(BF16)
        o_ref[...] = x

    @pl.when(first)
    def _():
        for s in range(n_sub):
            extg[s][0:hp, :] = pg_ref[0, :, cols(s)]
            extv[s][0:hp, :] = pv_ref[0, :, cols(s)]

    @pl.when(jnp.logical_not(first))
    def _():
        for s in range(n_sub):
            extg[s][0:hp, :] = carg_ref[c, :, cols(s)]
            extv[s][0:hp, :] = carv_ref[c, :, cols(s)]

    xn = xn_ref[...]

    def up(s):
        extg[s][hp:hp + tm, :] = _dot(xn, wug_ref[:, cols(s)])
        extv[s][hp:hp + tm, :] = _dot(xn, wuv_ref[:, cols(s)])

    def gated(s):
        yg = _conv_from_ext(extg[s], cwg_ref[:, cols(s)], cbg_ref[:, cols(s)], FFN_CONV, dil, tm, hp)
        yv = _conv_from_ext(extv[s], cwv_ref[:, cols(s)], cbv_ref[:, cols(s)], FFN_CONV, dil, tm, hp)
        return (_gelu(yg) * yv).astype(BF16)

    ahead = min(2, n_sub)
    for s in range(ahead):
        up(s)
    acc = None
    for s in range(n_sub):
        if s + ahead < n_sub:
            up(s + ahead)
        down = _dot(gated(s), wd_ref[cols(s), :])
        acc = down if acc is None else acc + down
        for ext, car_ref, s_ref in ((extg[s], carg_ref, sg_ref), (extv[s], carv_ref, sv_ref)):
            tail = ext[tm:tm + hp, :]
            car_ref[c, :, cols(s)] = tail
            s_ref[0, :, cols(s)] = tail
    o_ref[...] += acc


def _ffn_call(x, geom, gain, w_up, conv_w, conv_b, prev, w_down, block, ck):
    tm, hp = geom.tm, geom.halo(FFN_CONV)
    nblk = D_FF // block
    n_sub = block // ck
    tps = geom.tps
    kern = functools.partial(_ffn_kernel, tm=tm, tps=tps, dil=geom.dil, hp=hp, ck=ck)
    resident = {"pipeline_mode": pl.Buffered(1)} if nblk == 1 else {}
    col = lambda r, off, **kw: pl.BlockSpec((r, block), lambda i, c: (0, off + c), **kw)
    st_in = lambda off: pl.BlockSpec((1, hp, block), lambda i, c: (i // tps, 0, off + c))
    st_out = pl.BlockSpec((1, hp, block), lambda i, c: (i, 0, c))
    return pl.pallas_call(
        kern,
        grid=(geom.n_tiles, nblk),
        in_specs=[pl.BlockSpec((tm, D_MODEL), lambda i, c: (i, 0)),
                  _full_spec(gain.shape),
                  col(D_MODEL, 0, **resident), col(D_MODEL, nblk, **resident),
                  col(FFN_CONV, 0), col(FFN_CONV, nblk),
                  col(1, 0), col(1, nblk),
                  st_in(0), st_in(nblk),
                  pl.BlockSpec((block, D_MODEL), lambda i, c: (c, 0), **resident)],
        out_specs=[pl.BlockSpec((tm, D_MODEL), lambda i, c: (i, 0)), st_out, st_out],
        out_shape=[jax.ShapeDtypeStruct((geom.rows, D_MODEL), F32),
                   jax.ShapeDtypeStruct((geom.n_tiles, hp, D_FF), F32),
                   jax.ShapeDtypeStruct((geom.n_tiles, hp, D_FF), F32)],
        scratch_shapes=[pltpu.VMEM((tm, D_MODEL), BF16),
                        pltpu.VMEM((nblk, hp, block), F32), pltpu.VMEM((nblk, hp, block), F32)]
        + [pltpu.VMEM((hp + tm, ck), F32)] * (2 * n_sub),
        compiler_params=_params(2),
    )(x, gain, w_up, w_up, conv_w, conv_w, conv_b, conv_b, prev, prev, w_down)


def _state_to_halo(state, geom, hp):
    n, w1, ch = state.shape
    if geom.dil == 1:
        rows = state
    else:
        rows = state.transpose(1, 0, 2).reshape(1, w1 * n, ch)
    return jnp.pad(rows, ((0, 0), (hp - rows.shape[1], 0), (0, 0)))


def _halo_to_state(halo, geom, n, w1):
    ch = halo.shape[-1]
    if halo.shape[0] != geom.n_seq:
        halo = halo[geom.tps - 1::geom.tps]
    if geom.dil == 1:
        return halo[:, halo.shape[1] - w1:, :]
    return halo[0, halo.shape[1] - w1 * n:, :].reshape(w1, n, ch).transpose(1, 0, 2)


def _block_diag(w):
    nb, bi, bj = w.shape
    eye = jnp.eye(nb, dtype=w.dtype)
    return (eye[:, None, :, None] * w[:, :, None, :]).reshape(nb * bi, nb * bj)


def _trunk(x_rows, geom, n, pos0, p, paged, rg_h, rg_conv, pool_buf, ffn_buf, ffn_ck):
    depth = p["norm_mix"].shape[0]
    ks, vs, lfs, hs, cs, pbs, fbs = [], [], [], [], [], [], []
    dil = geom.dil
    t_steps = geom.rows // n
    dilp = _round_up(dil, SUBLANES)
    x = x_rows
    for layer in range(depth):
        li = layer // 2
        if layer % 2 == 0:
            w_in = p["ab_w_in"][li]
            a3 = 3 * A_WIDTH
            w = {
                "g": p["norm_mix"][layer][None, :],
                "wqk": w_in[:, :2 * A_WIDTH].astype(BF16),
                "wv": w_in[:, 2 * A_WIDTH:a3].astype(BF16),
                "wf": jnp.pad(w_in[:, a3:a3 + FOX_HEADS], ((0, 0), (0, F_PAD - FOX_HEADS))).astype(BF16),
                "wrg": w_in[:, a3 + FOX_HEADS:].astype(BF16),
                "bf": jnp.pad(p["ab_b_f"][li], (0, F_PAD - FOX_HEADS))[None, :],
                "qg": jnp.tile(p["ab_q_gain"][li], FOX_HEADS)[None, :],
                "kg": jnp.tile(p["ab_k_gain"][li], FOX_HEADS)[None, :],
                "bd": _block_diag(jnp.full((FOX_HEADS, HEAD_DIM, HEAD_DIM), 1.0 / HEAD_DIM, F32)).astype(BF16),
                "cw": p["ab_conv_w"][li], "cb": p["ab_conv_b"][li][None, :],
                "wa": _block_diag(p["ab_w_a"][li]).astype(BF16), "ba": p["ab_b_a"][li][None, :],
                "wx": _block_diag(p["ab_w_x"][li]).astype(BF16), "bx": p["ab_b_x"][li][None, :],
                "lam": p["ab_lambda"][li][None, :],
            }
            hp = geom.halo(RG_CONV)
            cprev = _state_to_halo(rg_conv[li], geom, hp)
            if dil == 1:
                h0 = jnp.pad(rg_h[li][:, None, :], ((0, 0), (0, dilp - 1), (0, 0)))
            else:
                h0 = jnp.pad(rg_h[li][None], ((0, 0), (0, dilp - dil), (0, 0)))
            scale = HEAD_DIM ** -0.5
            q_mult = scale * LOG2E if paged is None else scale
            q, k, v, kt, vt, lft, yrg, cst, hl = _ab_in_call(x, geom, w, cprev, h0, q_mult)

            if paged is None:
                cq, ck = _cumsum_call(lft.reshape(n * FOX_HEADS, t_steps))
                cq = cq.reshape(n, FOX_HEADS, GATE_ROWS, t_steps)
                ck = ck.reshape(n, FOX_HEADS, GATE_ROWS, t_steps)
                heads_t = lambda z: z.reshape(n, FOX_HEADS, HEAD_DIM, t_steps)
                attn = _attn_prompt_call(heads_t(q), cq, heads_t(k), ck, heads_t(v), min(512, t_steps))
                attn = attn.reshape(n, A_WIDTH, t_steps)
                ks.append(kt.reshape(n, FOX_HEADS, HEAD_DIM, t_steps).transpose(0, 3, 1, 2))
                vs.append(vt.reshape(n, FOX_HEADS, HEAD_DIM, t_steps).transpose(0, 3, 1, 2))
                lfs.append(lft.transpose(0, 2, 1))
                hs.append(hl[:, 0, :])
            else:
                ckt, cvt, clft, page_table = paged

                def heads_bm(z):
                    return z.reshape(t_steps, n, FOX_HEADS, HEAD_DIM).transpose(1, 2, 0, 3)
                lf_new = lft.transpose(2, 1, 0)[:, :, None, :]
                attn = _attn_sample_call(page_table, heads_bm(q), heads_bm(k), heads_bm(v), lf_new,
                                         ckt, cvt, clft, li)
                attn = attn.transpose(2, 0, 1, 3).reshape(geom.rows, A_WIDTH)
                ks.append(kt.reshape(t_steps, FOX_HEADS, HEAD_DIM, n).transpose(3, 0, 1, 2))
                vs.append(vt.reshape(t_steps, FOX_HEADS, HEAD_DIM, n).transpose(3, 0, 1, 2))
                lfs.append(lft.transpose(2, 0, 1))
                hs.append(hl[0, :dil, :])
            cs.append(_halo_to_state(cst, geom, n, RG_CONV - 1))
            w_out = p["ab_w_out"][li].astype(BF16)
            x = _ab_out_call(x, attn, yrg, w_out[:A_WIDTH], w_out[A_WIDTH:], geom)
        else:
            hp = geom.halo(POOL_BUF + 1)
            prev = _state_to_halo(pool_buf[li], geom, hp)
            x, st = _pool_call(x, geom, p["norm_mix"][layer][None, :], prev,
                               p["pool_w"][li].astype(BF16), p["pool_scale"][li][None, :], pos0)
            pbs.append(_halo_to_state(st, geom, n, POOL_BUF))
        hp = geom.halo(FFN_CONV)
        prev = _state_to_halo(ffn_buf[layer], geom, hp)
        x, sg, sv = _ffn_call(x, geom, p["norm_ffn"][layer][None, :], p["ffn_w_up"][layer].astype(BF16),
                              p["ffn_conv_w"][layer], p["ffn_conv_b"][layer][None, :], prev,
                              p["ffn_w_down"][layer].astype(BF16), *ffn_ck)
        fbs.append(_halo_to_state(jnp.concatenate([sg, sv], axis=-1), geom, n, FFN_CONV - 1))
    return x, (jnp.stack(ks), jnp.stack(vs), jnp.stack(lfs), jnp.stack(hs), jnp.stack(cs),
               jnp.stack(pbs), jnp.stack(fbs))


def kernel(x_prompt, x_sample, cache_k, cache_v, cache_logf, state_rg_h, state_rg_conv, state_pool, state_ffn_conv, page_table, norm_mix, norm_ffn, ab_w_in, ab_b_f, ab_q_gain, ab_k_gain, ab_conv_w, ab_conv_b, ab_w_a, ab_b_a, ab_w_x, ab_b_x, ab_lambda, ab_w_out, pool_w, pool_scale, ffn_w_up, ffn_conv_w, ffn_conv_b, ffn_w_down):
    p = {
        "norm_mix": norm_mix, "norm_ffn": norm_ffn,
        "ab_w_in": ab_w_in, "ab_b_f": ab_b_f, "ab_q_gain": ab_q_gain, "ab_k_gain": ab_k_gain,
        "ab_conv_w": ab_conv_w, "ab_conv_b": ab_conv_b, "ab_w_a": ab_w_a, "ab_b_a": ab_b_a,
        "ab_w_x": ab_w_x, "ab_b_x": ab_b_x, "ab_lambda": ab_lambda, "ab_w_out": ab_w_out,
        "pool_w": pool_w, "pool_scale": pool_scale,
        "ffn_w_up": ffn_w_up, "ffn_conv_w": ffn_conv_w, "ffn_conv_b": ffn_conv_b, "ffn_w_down": ffn_w_down,
    }
    depth = norm_mix.shape[0]
    n_ab, n_pool = (depth + 1) // 2, depth // 2

    bsz, t, _ = x_prompt.shape
    geom_p = _Geom(bsz, t, 1, 512)
    y_p, st_p = _trunk(
        x_prompt.reshape(bsz * t, D_MODEL), geom_p, bsz, 0, p, None,
        jnp.zeros((n_ab, bsz, B_WIDTH), F32), jnp.zeros((n_ab, bsz, RG_CONV - 1, B_WIDTH), F32),
        jnp.zeros((n_pool, bsz, POOL_BUF, D_MODEL), F32),
        jnp.zeros((depth, bsz, FFN_CONV - 1, 2 * D_FF), F32), (D_FF, 1024))
    y_prompt = y_p.reshape(bsz, t, D_MODEL)

    db, dt, _ = x_sample.shape
    n_pool_pages, page = cache_k.shape[1], cache_k.shape[2]
    past_len = page_table.shape[1] * page
    geom_s = _Geom(1, dt * db, db, 512)
    paged = (cache_k.transpose(0, 1, 3, 4, 2), cache_v.transpose(0, 1, 3, 4, 2),
             cache_logf.transpose(0, 1, 3, 2), page_table)
    y_s, st_s = _trunk(
        x_sample.transpose(1, 0, 2).reshape(dt * db, D_MODEL), geom_s, db, past_len, p, paged,
        state_rg_h, state_rg_conv, state_pool, state_ffn_conv, (512, 512))
    y_sample = y_s.reshape(dt, db, D_MODEL).transpose(1, 0, 2)
    return (y_prompt, y_sample) + st_p + st_s
```

```python
import functools

import jax
import jax.numpy as jnp
from jax import lax
from jax.experimental import pallas as pl
from jax.experimental.pallas import tpu as pltpu

D_MODEL = 1024
A_WIDTH = 512
B_WIDTH = 512
HEAD_DIM = 64
FOX_HEADS = 8
RG_CONV = 4
RG_C = 8.0
POOL_WINDOWS = (2, 4, 8, 16)
POOL_GD = 256
POOL_BUF = 15
D_FF = 3072
FFN_CONV = 3
EPS = 1e-6
NEG = -1e30
F_PAD = 128
GATE_ROWS = 16
ATTN_HEADS_PER_STEP = 2
LOG2E = 1.4426950408889634
SUBLANES = 8
MXU_TILE = 256
F32 = jnp.float32
BF16 = jnp.bfloat16
VMEM_LIMIT_BYTES = 56 * 1024 * 1024


def _round_up(x, m):
    return -(-x // m) * m


def _full_spec(shape):
    return pl.BlockSpec(shape, lambda *_: (0,) * len(shape))


def _params(n_axes):
    return pltpu.CompilerParams(dimension_semantics=("arbitrary",) * n_axes,
                                vmem_limit_bytes=VMEM_LIMIT_BYTES)


def _dot(a, b):
    return jnp.dot(a, b, preferred_element_type=F32)


def _dot_nt(a, b):
    return lax.dot_general(a, b, (((1,), (1,)), ((), ())), preferred_element_type=F32)


def _rms(x, gain):
    y = x * lax.rsqrt(jnp.mean(x * x, axis=-1, keepdims=True) + EPS)
    return y * gain


def _gelu(x):
    return 0.5 * x * (1.0 + jnp.tanh(0.7978845608028654 * (x + 0.044715 * (x * x * x))))


def _softplus(x):
    return jnp.maximum(x, 0.0) + jnp.log1p(jnp.exp(-jnp.abs(x)))


class _Geom:
    def __init__(self, n_seq, rows_per_seq, dil, tile):
        self.n_seq, self.rows_per_seq, self.dil = n_seq, rows_per_seq, dil
        self.tm = min(tile, rows_per_seq)
        assert rows_per_seq % self.tm == 0 and self.tm % dil == 0 and self.tm % SUBLANES == 0
        self.tps = rows_per_seq // self.tm
        self.rows = n_seq * rows_per_seq
        self.n_tiles = self.rows // self.tm

    def halo(self, width):
        return _round_up((width - 1) * self.dil, SUBLANES)


def _conv_from_ext(ext_ref, w_ref, bias, width, dil, tm, hp):
    y = bias
    for j in range(width):
        off = hp - (width - 1 - j) * dil
        y = y + ext_ref[off:off + tm, :] * w_ref[j:j + 1, :]
    return y


def _ab_in_kernel(x_ref, g_ref, wqk_ref, wv_ref, wrg_ref, wf_ref, bf_ref, qg_ref, kg_ref, bd_ref,
                  cw_ref, cb_ref, wa_ref, ba_ref, wx_ref, bx_ref, lam_ref, cprev_ref, h0_ref,
                  q_out, k_out, v_out, kt_out, vt_out, lft_out, y_out, cst_out, h_out,
                  ext_ref, hc_ref, *, tm, tps, dil, hp, q_mult):
    i = pl.program_id(0)
    first = (i % tps) == 0
    xn = _rms(x_ref[...], g_ref[...]).astype(BF16)

    bd = bd_ref[...]

    def head_norm(z, gain):
        z2 = (z * z).astype(BF16)
        wide = bd.shape[0]
        ms = jnp.concatenate([_dot(z2[:, c:c + wide], bd) for c in range(0, A_WIDTH, wide)], axis=1)
        return z * lax.rsqrt(ms + EPS) * gain

    qk = _dot(xn, wqk_ref[...])
    q = head_norm(qk[:, :A_WIDTH], qg_ref[...])
    k = head_norm(qk[:, A_WIDTH:], kg_ref[...])
    v = _dot(xn, wv_ref[...])
    q = q * q_mult

    f = _dot(xn, wf_ref[...]) + bf_ref[...]
    lf = jnp.minimum(f, 0.0) - jnp.log1p(jnp.exp(-jnp.abs(f)))

    unit = tm if dil == 1 else dil
    for u in range(tm // unit):
        rows_u = slice(u * unit, (u + 1) * unit)
        kt = k[rows_u, :].T
        vt = v[rows_u, :].T
        kt_out[u] = kt
        vt_out[u] = vt
        lft_out[u] = lf[rows_u, :].T[:FOX_HEADS, :]
        if dil == 1:
            q_out[u] = q.T.astype(BF16)
            k_out[u] = kt.astype(BF16)
            v_out[u] = vt.astype(BF16)
    if dil != 1:
        q_out[...] = q.astype(BF16)
        k_out[...] = k.astype(BF16)
        v_out[...] = v.astype(BF16)

    rg = _dot(xn, wrg_ref[...])
    xr = rg[:, :B_WIDTH]
    gate = rg[:, B_WIDTH:]

    @pl.when(first)
    def _():
        ext_ref[0:hp, :] = cprev_ref[0]
        hc_ref[...] = h0_ref[0]

    ext_ref[hp:hp + tm, :] = xr
    xc = _conv_from_ext(ext_ref, cw_ref, cb_ref[...], RG_CONV, dil, tm, hp)
    tail = ext_ref[tm:tm + hp, :]
    ext_ref[0:hp, :] = tail
    cst_out[0] = tail

    xcb = xc.astype(BF16)
    r = jax.nn.sigmoid(_dot(xcb, wa_ref[...]) + ba_ref[...])
    gi = jax.nn.sigmoid(_dot(xcb, wx_ref[...]) + bx_ref[...])
    log_a = -RG_C * r * _softplus(-lam_ref[...])
    a = jnp.exp(log_a)
    inp = jnp.sqrt(-jnp.tanh(log_a) * (a * a + 1.0)) * gi * xc

    rows = lax.broadcasted_iota(jnp.int32, (tm, 1), 0)
    s = dil
    while s < tm:
        valid = rows >= s
        a_sh = jnp.where(valid, pltpu.roll(a, s, 0), 1.0)
        h_sh = jnp.where(valid, pltpu.roll(inp, s, 0), 0.0)
        inp = a * h_sh + inp
        a = a * a_sh
        s *= 2

    hc = hc_ref[0:dil, :]
    if dil == 1:
        hc_rows = jnp.broadcast_to(hc, (tm, B_WIDTH))
    else:
        hc_rows = jnp.concatenate([hc] * (tm // dil), axis=0)
    h = inp + a * hc_rows
    hc_ref[0:dil, :] = h[tm - dil:tm, :]
    h_out[0] = hc_ref[...]
    y_out[...] = (h * _gelu(gate)).astype(BF16)


def _ab_in_call(x, geom, w, cprev, h0, q_mult):
    tm, hp, tps = geom.tm, geom.halo(RG_CONV), geom.tps
    dilp = _round_up(geom.dil, SUBLANES)
    rows = geom.rows
    tile = lambda n: pl.BlockSpec((tm, n), lambda i: (i, 0))
    seq3 = lambda r, n: pl.BlockSpec((1, r, n), lambda i: (i // tps, 0, 0))
    if geom.dil == 1:
        t_shape = lambda ch: (geom.n_seq, ch, geom.rows_per_seq)
        t_spec = lambda ch: pl.BlockSpec((1, ch, tm), lambda i: (i // tps, 0, i % tps))
        op_spec, op_shape = t_spec(A_WIDTH), jax.ShapeDtypeStruct(t_shape(A_WIDTH), BF16)
    else:
        t_shape = lambda ch: (rows // geom.dil, ch, geom.dil)
        t_spec = lambda ch: pl.BlockSpec((tm // geom.dil, ch, geom.dil), lambda i: (i, 0, 0))
        op_spec, op_shape = tile(A_WIDTH), jax.ShapeDtypeStruct((rows, A_WIDTH), BF16)
    consts = [w["g"], w["wqk"], w["wv"], w["wrg"], w["wf"], w["bf"], w["qg"], w["kg"], w["bd"],
              w["cw"], w["cb"], w["wa"], w["ba"], w["wx"], w["bx"], w["lam"]]
    kern = functools.partial(_ab_in_kernel, tm=tm, tps=tps, dil=geom.dil, hp=hp, q_mult=q_mult)
    return pl.pallas_call(
        kern,
        grid=(geom.n_tiles,),
        in_specs=[tile(D_MODEL)] + [_full_spec(c.shape) for c in consts]
        + [seq3(hp, B_WIDTH), seq3(dilp, B_WIDTH)],
        out_specs=[op_spec, op_spec, op_spec,
                   t_spec(A_WIDTH), t_spec(A_WIDTH), t_spec(FOX_HEADS), tile(B_WIDTH),
                   seq3(hp, B_WIDTH), seq3(dilp, B_WIDTH)],
        out_shape=[op_shape, op_shape, op_shape,
                   jax.ShapeDtypeStruct(t_shape(A_WIDTH), F32),
                   jax.ShapeDtypeStruct(t_shape(A_WIDTH), F32),
                   jax.ShapeDtypeStruct(t_shape(FOX_HEADS), F32),
                   jax.ShapeDtypeStruct((rows, B_WIDTH), BF16),
                   jax.ShapeDtypeStruct((geom.n_seq, hp, B_WIDTH), F32),
                   jax.ShapeDtypeStruct((geom.n_seq, dilp, B_WIDTH), F32)],
        scratch_shapes=[pltpu.VMEM((hp + tm, B_WIDTH), F32), pltpu.VMEM((dilp, B_WIDTH), F32)],
        compiler_params=_params(1),
    )(x, *consts, cprev, h0)


def _cumsum_kernel(x_ref, cq_ref, ck_ref, *, n):
    x = x_ref[...]
    lane = lax.broadcasted_iota(jnp.int32, x.shape, 1)
    s = 1
    while s < n:
        x = x + jnp.where(lane >= s, pltpu.roll(x, s, 1), 0.0)
        s *= 2
    x = x * LOG2E
    hi = x.astype(BF16).astype(F32)
    rest = x - hi
    mid = rest.astype(BF16).astype(F32)
    lo = rest - mid
    row = lax.broadcasted_iota(jnp.int32, (GATE_ROWS, n), 0)
    for r in range(x.shape[0]):
        parts = [jnp.broadcast_to(p[r:r + 1, :], (GATE_ROWS, n)) for p in (hi, mid, lo)]
        cq = jnp.where(row < 3, -1.0, 0.0)
        ck = jnp.where((row >= 3) & (row < 6), 1.0, 0.0)
        for j, part in enumerate(parts):
            cq = jnp.where(row == 3 + j, part, cq)
            ck = jnp.where(row == j, part, ck)
        cq_ref[r] = cq.astype(BF16)
        ck_ref[r] = ck.astype(BF16)


def _cumsum_call(x):
    rows, n = x.shape
    out = jax.ShapeDtypeStruct((rows, GATE_ROWS, n), BF16)
    return pl.pallas_call(
        functools.partial(_cumsum_kernel, n=n),
        grid=(1,),
        in_specs=[_full_spec(x.shape)],
        out_specs=[_full_spec(out.shape)] * 2,
        out_shape=[out] * 2,
        compiler_params=_params(1),
    )(x)


def _attn_prompt_kernel(qt_ref, cq_ref, kt_ref, ck_ref, vt_ref, o_ref, *, tq, sb, depth):
    qi = pl.program_id(2)
    n_heads = qt_ref.shape[1]
    heads = range(n_heads)
    qt = [jnp.concatenate([qt_ref[0, h], cq_ref[0, h]], axis=0) for h in heads]
    n_sb = tq // sb
    row = lax.broadcasted_iota(jnp.int32, (sb, tq), 0)
    col = lax.broadcasted_iota(jnp.int32, (sb, tq), 1)

    def key_rows(j, s):
        return pl.ds(pl.multiple_of(j * tq + s * sb, sb), sb)

    def scores(h, j, s):
        kt = jnp.concatenate([kt_ref[0, h, :, key_rows(j, s)], ck_ref[0, h, :, key_rows(j, s)]], axis=0)
        return lax.dot_general(kt, qt[h], (((0,), (0,)), ((), ())), preferred_element_type=F32)

    def chunk(j, carry, diagonal):
        state, ahead = list(carry[0]), [list(a) for a in carry[1]]
        for s in range(n_sb):
            for h in heads:
                m, l, acc = state[h]
                st = ahead[h].pop(0)
                if s + depth < n_sb:
                    ahead[h].append(scores(h, j, s + depth))
                elif not diagonal:
                    ahead[h].append(scores(h, j + 1, s + depth - n_sb))
                if diagonal:
                    st = jnp.where(row + s * sb <= col, st, NEG)
                m_new = jnp.maximum(m, jnp.max(st, axis=0, keepdims=True))
                alpha = jnp.exp2(m - m_new)
                p = jnp.exp2(st - m_new)
                l = alpha * l + jnp.sum(p, axis=0, keepdims=True)
                acc = alpha * acc + _dot(vt_ref[0, h, :, key_rows(j, s)], p.astype(BF16))
                state[h] = (m_new, l, acc)
        return tuple(state), tuple(tuple(a) for a in ahead)

    state = tuple((jnp.full((1, tq), NEG, F32), jnp.zeros((1, tq), F32), jnp.zeros((HEAD_DIM, tq), F32))
                  for _ in heads)
    carry = (state, tuple(tuple(scores(h, 0, s) for s in range(depth)) for h in heads))
    carry = lax.fori_loop(0, qi, lambda j, c: chunk(j, c, False), carry)
    state, _ = chunk(qi, carry, True)
    for h, (_, l, acc) in enumerate(state):
        o_ref[0, h] = (acc / l).astype(BF16)


def _attn_prompt_call(q_t, cq, k_t, ck, v_t, tq):
    bsz, nh, dh, t = q_t.shape
    sb = min(128, tq)
    depth = min(2, tq // sb)
    hg = ATTN_HEADS_PER_STEP
    tile = lambda r: pl.BlockSpec((1, hg, r, tq), lambda b, h, i: (b, h, 0, i))
    whole = lambda r: pl.BlockSpec((1, hg, r, t), lambda b, h, i: (b, h, 0, 0))
    return pl.pallas_call(
        functools.partial(_attn_prompt_kernel, tq=tq, sb=sb, depth=depth),
        grid=(bsz, nh // hg, t // tq),
        in_specs=[tile(dh), tile(GATE_ROWS), whole(dh), whole(GATE_ROWS), whole(dh)],
        out_specs=tile(dh),
        out_shape=jax.ShapeDtypeStruct((bsz, nh, dh, t), BF16),
        compiler_params=_params(3),
    )(q_t, cq, k_t, ck, v_t)


def _attn_sample_kernel(pt_ref, q_ref, kn_ref, vn_ref, lfn_ref, *refs, n_pages, page, dt):
    del pt_ref
    k_refs = refs[:n_pages]
    v_refs = refs[n_pages:2 * n_pages]
    lf_refs = refs[2 * n_pages:3 * n_pages]
    o_ref = refs[3 * n_pages]
    nh = FOX_HEADS
    past = n_pages * page
    bdot = lambda a, b: lax.dot_general(a, b, (((2,), (1,)), ((0,), (0,))), preferred_element_type=F32)
    bdot_nt = lambda a, b: lax.dot_general(a, b, (((2,), (2,)), ((0,), (0,))), preferred_element_type=F32)

    q3 = q_ref[0]

    lft = jnp.concatenate([r[0, 0] for r in lf_refs], axis=1)
    lane = lax.broadcasted_iota(jnp.int32, lft.shape, 1)
    suf = lft
    s = 1
    while s < past:
        suf = suf + jnp.where(lane < past - s, pltpu.roll(suf, past - s, 1), 0.0)
        s *= 2
    suf = suf - lft
    suf3 = jnp.stack([jnp.broadcast_to(suf[h:h + 1, :], (dt, past)) for h in range(nh)], axis=0)

    lnew = jnp.broadcast_to(lfn_ref[0], (nh, dt, dt))
    colq = lax.broadcasted_iota(jnp.int32, (nh, dt, dt), 2)
    tq = lax.broadcasted_iota(jnp.int32, (nh, dt, dt), 1)
    causal = colq <= tq
    nq = jnp.sum(jnp.where(causal, lnew, 0.0), axis=2, keepdims=True)
    g = jnp.zeros((nh, dt, dt), F32)
    for l in range(dt):
        g = g + jnp.where(colq >= l, lnew[:, :, l:l + 1], 0.0)

    kt_all = jnp.concatenate([r[0, 0].astype(BF16) for r in k_refs], axis=2)
    vt_all = jnp.concatenate([r[0, 0].astype(BF16) for r in v_refs], axis=2)
    s_past = bdot(q3, kt_all) + (suf3 + nq)
    s_new = jnp.where(causal, bdot_nt(q3, kn_ref[0]) + (nq - g), NEG)

    m = jnp.maximum(jnp.max(s_new, axis=2, keepdims=True), jnp.max(s_past, axis=2, keepdims=True))
    p_new = jnp.exp(s_new - m)
    p_past = jnp.exp(s_past - m)
    l = jnp.sum(p_new, axis=2, keepdims=True) + jnp.sum(p_past, axis=2, keepdims=True)
    acc = bdot(p_new.astype(BF16), vn_ref[0]) + bdot_nt(p_past.astype(BF16), vt_all)
    o_ref[0] = (acc / l).astype(BF16)


def _attn_sample_call(page_table, q, k_new, v_new, lf_new, cache_kt, cache_vt, cache_lft, layer):
    db, nh, dt, dh = q.shape
    n_pages = page_table.shape[1]
    page = cache_kt.shape[-1]
    new_spec = pl.BlockSpec((1, nh, dt, dh), lambda b, pt: (b, 0, 0, 0))

    def page_spec(shape, p):
        zeros = (0,) * len(shape)
        return pl.BlockSpec((1, 1) + shape, lambda b, pt: (layer, pt[b, p]) + zeros)

    in_specs = ([new_spec, new_spec, new_spec,
                 pl.BlockSpec((1, nh, 1, dt), lambda b, pt: (b, 0, 0, 0))]
                + [page_spec((nh, dh, page), p) for p in range(n_pages)]
                + [page_spec((nh, dh, page), p) for p in range(n_pages)]
                + [page_spec((nh, page), p) for p in range(n_pages)])
    grid_spec = pltpu.PrefetchScalarGridSpec(
        num_scalar_prefetch=1, grid=(db,), in_specs=in_specs, out_specs=new_spec)
    return pl.pallas_call(
        functools.partial(_attn_sample_kernel, n_pages=n_pages, page=page, dt=dt),
        grid_spec=grid_spec,
        out_shape=jax.ShapeDtypeStruct((db, nh, dt, dh), BF16),
        compiler_params=_params(1),
    )(page_table, q, k_new, v_new, lf_new,
      *([cache_kt] * n_pages), *([cache_vt] * n_pages), *([cache_lft] * n_pages))


def _pool_kernel(x_ref, gain_ref, prev_ref, w_ref, scale_ref, o_ref, st_ref, ext_ref,
                 *, tm, tps, dil, hp, pos0):
    i = pl.program_id(0)
    first = (i % tps) == 0
    x = x_ref[...]
    xn = _rms(x, gain_ref[...])
    rows = lax.broadcasted_iota(jnp.int32, (tm, 1), 0) + (i % tps) * tm
    pos = lax.div(rows, jnp.int32(dil)) + pos0

    @pl.when(first)
    def _():
        ext_ref[0:hp, :] = prev_ref[0]

    ext_ref[hp:hp + tm, :] = xn
    for gi, window in enumerate(POOL_WINDOWS):
        cols = slice(gi * POOL_GD, (gi + 1) * POOL_GD)
        xg = xn[:, cols]
        total = xg
        for j in range(1, window):
            total = total + ext_ref[hp - j * dil:hp - j * dil + tm, cols]
        cnt = jnp.minimum(pos + 1, window).astype(F32)
        diff = total / cnt - xg
        y = _dot(diff.astype(BF16), w_ref[gi]) * scale_ref[:, cols]
        o_ref[:, cols] = x[:, cols] + y
    tail = ext_ref[tm:tm + hp, :]
    ext_ref[0:hp, :] = tail
    st_ref[0] = tail


def _pool_call(x, geom, gain, prev, w, scale, pos0):
    tm, hp = geom.tm, geom.halo(POOL_BUF + 1)
    kern = functools.partial(_pool_kernel, tm=tm, tps=geom.tps, dil=geom.dil, hp=hp, pos0=pos0)
    resident = {"pipeline_mode": pl.Buffered(1)} if geom.n_seq == 1 else {}
    return pl.pallas_call(
        kern,
        grid=(geom.n_tiles,),
        in_specs=[pl.BlockSpec((tm, D_MODEL), lambda i: (i, 0)),
                  _full_spec(gain.shape),
                  pl.BlockSpec((1, hp, D_MODEL), lambda i: (i // geom.tps, 0, 0), **resident),
                  _full_spec(w.shape),
                  _full_spec(scale.shape)],
        out_specs=[pl.BlockSpec((tm, D_MODEL), lambda i: (i, 0)),
                   pl.BlockSpec((1, hp, D_MODEL), lambda i: (i, 0, 0))],
        out_shape=[jax.ShapeDtypeStruct((geom.rows, D_MODEL), F32),
                   jax.ShapeDtypeStruct((geom.n_tiles, hp, D_MODEL), F32)],
        scratch_shapes=[pltpu.VMEM((hp + tm, D_MODEL), F32)],
        compiler_params=_params(1),
    )(x, gain, prev, w, scale)


def _ffn_kernel(x_ref, *refs, tm, tps, dil, hp, ck, mixer):
    if mixer is not None:
        a_ref, y_ref, wa_ref, wy_ref = refs[:4]
        refs = refs[4:]
    (g_ref, wug_ref, wuv_ref, cwg_ref, cwv_ref, cbg_ref, cbv_ref, pg_ref, pv_ref, wd_ref,
     o_ref, sg_ref, sv_ref, xn_ref, carg_ref, carv_ref) = refs[:16]
    ext_refs = refs[16:]
    i = pl.program_id(0)
    c = pl.program_id(1)
    first = (i % tps) == 0
    n_sub = len(ext_refs) // 2
    extg, extv = ext_refs[:n_sub], ext_refs[n_sub:]
    cols = lambda s: slice(s * ck, (s + 1) * ck)

    @pl.when(c == 0)
    def _():
        x = x_ref[...]
        if mixer == "transposed":
            x = x + lax.dot_general(a_ref[0], wa_ref[...], (((0,), (0,)), ((), ())),
                                    preferred_element_type=F32)
        elif mixer == "rows":
            x = x + _dot(a_ref[...], wa_ref[...])
        if mixer is not None:
            x = x + _dot(y_ref[...], wy_ref[...])
        xn_ref[...] = _rms(x, g_ref[...]).astype(BF16)
        o_ref[...] = x

    @pl.when(first)
    def _():
        for s in range(n_sub):
            extg[s][0:hp, :] = pg_ref[0, :, cols(s)]
            extv[s][0:hp, :] = pv_ref[0, :, cols(s)]

    @pl.when(jnp.logical_not(first))
    def _():
        for s in range(n_sub):
            extg[s][0:hp, :] = carg_ref[c, :, cols(s)]
            extv[s][0:hp, :] = carv_ref[c, :, cols(s)]

    xn = xn_ref[...]

    def up(s):
        extg[s][hp:hp + tm, :] = _dot(xn, wug_ref[:, cols(s)])
        extv[s][hp:hp + tm, :] = _dot(xn, wuv_ref[:, cols(s)])

    def gated(s):
        yg = _conv_from_ext(extg[s], cwg_ref[:, cols(s)], cbg_ref[:, cols(s)], FFN_CONV, dil, tm, hp)
        yv = _conv_from_ext(extv[s], cwv_ref[:, cols(s)], cbv_ref[:, cols(s)], FFN_CONV, dil, tm, hp)
        return (_gelu(yg) * yv).astype(BF16)

    ahead = min(2, n_sub)
    for s in range(ahead):
        up(s)
    acc = None
    for s in range(n_sub):
        if s + ahead < n_sub:
            up(s + ahead)
        down = _dot(gated(s), wd_ref[cols(s), :])
        acc = down if acc is None else acc + down
        for ext, car_ref, s_ref in ((extg[s], carg_ref, sg_ref), (extv[s], carv_ref, sv_ref)):
            tail = ext[tm:tm + hp, :]
            car_ref[c, :, cols(s)] = tail
            s_ref[0, :, cols(s)] = tail
    o_ref[...] += acc


def _ffn_call(x, geom, gain, w_up, conv_w, conv_b, prev, w_down, block, ck, mixer_out=None):
    tm, hp = geom.tm, geom.halo(FFN_CONV)
    nblk = D_FF // block
    n_sub = block // ck
    tps = geom.tps
    resident = {"pipeline_mode": pl.Buffered(1)} if nblk == 1 else {}
    col = lambda r, off, **kw: pl.BlockSpec((r, block), lambda i, c: (0, off + c), **kw)
    st_in = lambda off: pl.BlockSpec((1, hp, block), lambda i, c: (i // tps, 0, off + c))
    st_out = pl.BlockSpec((1, hp, block), lambda i, c: (i, 0, c))
    rows_spec = lambda n: pl.BlockSpec((tm, n), lambda i, c: (i, 0))
    mixer, mixer_args, mixer_specs = None, [], []
    if mixer_out is not None:
        attn, yrg, w_attn, w_rg = mixer_out
        mixer = "transposed" if attn.ndim == 3 else "rows"
        a_spec = (pl.BlockSpec((1, A_WIDTH, tm), lambda i, c: (i // tps, 0, i % tps))
                  if mixer == "transposed" else rows_spec(A_WIDTH))
        w_spec = lambda w: pl.BlockSpec(w.shape, lambda i, c: (0, 0), pipeline_mode=pl.Buffered(1))
        mixer_args = [attn, yrg, w_attn, w_rg]
        mixer_specs = [a_spec, rows_spec(B_WIDTH), w_spec(w_attn), w_spec(w_rg)]
    kern = functools.partial(_ffn_kernel, tm=tm, tps=tps, dil=geom.dil, hp=hp, ck=ck, mixer=mixer)
    return pl.pallas_call(
        kern,
        grid=(geom.n_tiles, nblk),
        in_specs=[rows_spec(D_MODEL)] + mixer_specs + [
                  _full_spec(gain.shape),
                  col(D_MODEL, 0, **resident), col(D_MODEL, nblk, **resident),
                  col(FFN_CONV, 0), col(FFN_CONV, nblk),
                  col(1, 0), col(1, nblk),
                  st_in(0), st_in(nblk),
                  pl.BlockSpec((block, D_MODEL), lambda i, c: (c, 0), **resident)],
        out_specs=[pl.BlockSpec((tm, D_MODEL), lambda i, c: (i, 0)), st_out, st_out],
        out_shape=[jax.ShapeDtypeStruct((geom.rows, D_MODEL), F32),
                   jax.ShapeDtypeStruct((geom.n_tiles, hp, D_FF), F32),
                   jax.ShapeDtypeStruct((geom.n_tiles, hp, D_FF), F32)],
        scratch_shapes=[pltpu.VMEM((tm, D_MODEL), BF16),
                        pltpu.VMEM((nblk, hp, block), F32), pltpu.VMEM((nblk, hp, block), F32)]
        + [pltpu.VMEM((hp + tm, ck), F32)] * (2 * n_sub),
        compiler_params=_params(2),
    )(x, *mixer_args, gain, w_up, w_up, conv_w, conv_w, conv_b, conv_b, prev, prev, w_down)


def _state_to_halo(state, geom, hp):
    n, w1, ch = state.shape
    if geom.dil == 1:
        rows = state
    else:
        rows = state.transpose(1, 0, 2).reshape(1, w1 * n, ch)
    return jnp.pad(rows, ((0, 0), (hp - rows.shape[1], 0), (0, 0)))


def _halo_to_state(halo, geom, n, w1):
    ch = halo.shape[-1]
    if halo.shape[0] != geom.n_seq:
        halo = halo[geom.tps - 1::geom.tps]
    if geom.dil == 1:
        return halo[:, halo.shape[1] - w1:, :]
    return halo[0, halo.shape[1] - w1 * n:, :].reshape(w1, n, ch).transpose(1, 0, 2)


def _block_diag(w):
    nb, bi, bj = w.shape
    eye = jnp.eye(nb, dtype=w.dtype)
    return (eye[:, None, :, None] * w[:, :, None, :]).reshape(nb * bi, nb * bj)


def _trunk(x_rows, geom, n, pos0, p, paged, rg_h, rg_conv, pool_buf, ffn_buf, ffn_ck):
    depth = p["norm_mix"].shape[0]
    ks, vs, lfs, hs, cs, pbs, fbs = [], [], [], [], [], [], []
    dil = geom.dil
    t_steps = geom.rows // n
    dilp = _round_up(dil, SUBLANES)
    x = x_rows
    for layer in range(depth):
        li = layer // 2
        if layer % 2 == 0:
            w_in = p["ab_w_in"][li]
            a3 = 3 * A_WIDTH
            w = {
                "g": p["norm_mix"][layer][None, :],
                "wqk": w_in[:, :2 * A_WIDTH].astype(BF16),
                "wv": w_in[:, 2 * A_WIDTH:a3].astype(BF16),
                "wf": jnp.pad(w_in[:, a3:a3 + FOX_HEADS], ((0, 0), (0, F_PAD - FOX_HEADS))).astype(BF16),
                "wrg": w_in[:, a3 + FOX_HEADS:].astype(BF16),
                "bf": jnp.pad(p["ab_b_f"][li], (0, F_PAD - FOX_HEADS))[None, :],
                "qg": jnp.tile(p["ab_q_gain"][li], FOX_HEADS)[None, :],
                "kg": jnp.tile(p["ab_k_gain"][li], FOX_HEADS)[None, :],
                "bd": _block_diag(jnp.full((MXU_TILE // HEAD_DIM, HEAD_DIM, HEAD_DIM), 1.0 / HEAD_DIM,
                                           F32)).astype(BF16),
                "cw": p["ab_conv_w"][li], "cb": p["ab_conv_b"][li][None, :],
                "wa": _block_diag(p["ab_w_a"][li]).astype(BF16), "ba": p["ab_b_a"][li][None, :],
                "wx": _block_diag(p["ab_w_x"][li]).astype(BF16), "bx": p["ab_b_x"][li][None, :],
                "lam": p["ab_lambda"][li][None, :],
            }
            hp = geom.halo(RG_CONV)
            cprev = _state_to_halo(rg_conv[li], geom, hp)
            if dil == 1:
                h0 = jnp.pad(rg_h[li][:, None, :], ((0, 0), (0, dilp - 1), (0, 0)))
            else:
                h0 = jnp.pad(rg_h[li][None], ((0, 0), (0, dilp - dil), (0, 0)))
            scale = HEAD_DIM ** -0.5
            q_mult = scale * LOG2E if paged is None else scale
            q, k, v, kt, vt, lft, yrg, cst, hl = _ab_in_call(x, geom, w, cprev, h0, q_mult)

            if paged is None:
                cq, ck = _cumsum_call(lft.reshape(n * FOX_HEADS, t_steps))
                cq = cq.reshape(n, FOX_HEADS, GATE_ROWS, t_steps)
                ck = ck.reshape(n, FOX_HEADS, GATE_ROWS, t_steps)
                heads_t = lambda z: z.reshape(n, FOX_HEADS, HEAD_DIM, t_steps)
                attn = _attn_prompt_call(heads_t(q), cq, heads_t(k), ck, heads_t(v), min(512, t_steps))
                attn = attn.reshape(n, A_WIDTH, t_steps)
                ks.append(kt.reshape(n, FOX_HEADS, HEAD_DIM, t_steps).transpose(0, 3, 1, 2))
                vs.append(vt.reshape(n, FOX_HEADS, HEAD_DIM, t_steps).transpose(0, 3, 1, 2))
                lfs.append(lft.transpose(0, 2, 1))
                hs.append(hl[:, 0, :])
            else:
                ckt, cvt, clft, page_table = paged

                def heads_bm(z):
                    return z.reshape(t_steps, n, FOX_HEADS, HEAD_DIM).transpose(1, 2, 0, 3)
                lf_new = lft.transpose(2, 1, 0)[:, :, None, :]
                attn = _attn_sample_call(page_table, heads_bm(q), heads_bm(k), heads_bm(v), lf_new,
                                         ckt, cvt, clft, li)
                attn = attn.transpose(2, 0, 1, 3).reshape(geom.rows, A_WIDTH)
                ks.append(kt.reshape(t_steps, FOX_HEADS, HEAD_DIM, n).transpose(3, 0, 1, 2))
                vs.append(vt.reshape(t_steps, FOX_HEADS, HEAD_DIM, n).transpose(3, 0, 1, 2))
                lfs.append(lft.transpose(2, 0, 1))
                hs.append(hl[0, :dil, :])
            cs.append(_halo_to_state(cst, geom, n, RG_CONV - 1))
            w_out = p["ab_w_out"][li].astype(BF16)
            mixer_out = (attn, yrg, w_out[:A_WIDTH], w_out[A_WIDTH:])
        else:
            mixer_out = None
            hp = geom.halo(POOL_BUF + 1)
            prev = _state_to_halo(pool_buf[li], geom, hp)
            x, st = _pool_call(x, geom, p["norm_mix"][layer][None, :], prev,
                               p["pool_w"][li].astype(BF16), p["pool_scale"][li][None, :], pos0)
            pbs.append(_halo_to_state(st, geom, n, POOL_BUF))
        hp = geom.halo(FFN_CONV)
        prev = _state_to_halo(ffn_buf[layer], geom, hp)
        x, sg, sv = _ffn_call(x, geom, p["norm_ffn"][layer][None, :], p["ffn_w_up"][layer].astype(BF16),
                              p["ffn_conv_w"][layer], p["ffn_conv_b"][layer][None, :], prev,
                              p["ffn_w_down"][layer].astype(BF16), *ffn_ck, mixer_out=mixer_out)
        fbs.append(_halo_to_state(jnp.concatenate([sg, sv], axis=-1), geom, n, FFN_CONV - 1))
    return x, (jnp.stack(ks), jnp.stack(vs), jnp.stack(lfs), jnp.stack(hs), jnp.stack(cs),
               jnp.stack(pbs), jnp.stack(fbs))


def kernel(x_prompt, x_sample, cache_k, cache_v, cache_logf, state_rg_h, state_rg_conv, state_pool, state_ffn_conv, page_table, norm_mix, norm_ffn, ab_w_in, ab_b_f, ab_q_gain, ab_k_gain, ab_conv_w, ab_conv_b, ab_w_a, ab_b_a, ab_w_x, ab_b_x, ab_lambda, ab_w_out, pool_w, pool_scale, ffn_w_up, ffn_conv_w, ffn_conv_b, ffn_w_down):
    p = {
        "norm_mix": norm_mix, "norm_ffn": norm_ffn,
        "ab_w_in": ab_w_in, "ab_b_f": ab_b_f, "ab_q_gain": ab_q_gain, "ab_k_gain": ab_k_gain,
        "ab_conv_w": ab_conv_w, "ab_conv_b": ab_conv_b, "ab_w_a": ab_w_a, "ab_b_a": ab_b_a,
        "ab_w_x": ab_w_x, "ab_b_x": ab_b_x, "ab_lambda": ab_lambda, "ab_w_out": ab_w_out,
        "pool_w": pool_w, "pool_scale": pool_scale,
        "ffn_w_up": ffn_w_up, "ffn_conv_w": ffn_conv_w, "ffn_conv_b": ffn_conv_b, "ffn_w_down": ffn_w_down,
    }
    depth = norm_mix.shape[0]
    n_ab, n_pool = (depth + 1) // 2, depth // 2

    bsz, t, _ = x_prompt.shape
    geom_p = _Geom(bsz, t, 1, 512)
    y_p, st_p = _trunk(
        x_prompt.reshape(bsz * t, D_MODEL), geom_p, bsz, 0, p, None,
        jnp.zeros((n_ab, bsz, B_WIDTH), F32), jnp.zeros((n_ab, bsz, RG_CONV - 1, B_WIDTH), F32),
        jnp.zeros((n_pool, bsz, POOL_BUF, D_MODEL), F32),
        jnp.zeros((depth, bsz, FFN_CONV - 1, 2 * D_FF), F32), (D_FF, 1024))
    y_prompt = y_p.reshape(bsz, t, D_MODEL)

    db, dt, _ = x_sample.shape
    n_pool_pages, page = cache_k.shape[1], cache_k.shape[2]
    past_len = page_table.shape[1] * page
    geom_s = _Geom(1, dt * db, db, 512)
    paged = (cache_k.transpose(0, 1, 3, 4, 2), cache_v.transpose(0, 1, 3, 4, 2),
             cache_logf.transpose(0, 1, 3, 2), page_table)
    y_s, st_s = _trunk(
        x_sample.transpose(1, 0, 2).reshape(dt * db, D_MODEL), geom_s, db, past_len, p, paged,
        state_rg_h, state_rg_conv, state_pool, state_ffn_conv, (512, 512))
    y_sample = y_s.reshape(dt, db, D_MODEL).transpose(1, 0, 2)
    return (y_prompt, y_sample) + st_p + st_s
```

```python
import functools

import jax
import jax.numpy as jnp
from jax import lax
from jax.experimental import pallas as pl
from jax.experimental.pallas import tpu as pltpu

D_MODEL = 1024
A_WIDTH = 512
B_WIDTH = 512
HEAD_DIM = 64
FOX_HEADS = 8
RG_CONV = 4
RG_C = 8.0
POOL_WINDOWS = (2, 4, 8, 16)
POOL_GD = 256
POOL_BUF = 15
D_FF = 3072
FFN_CONV = 3
EPS = 1e-6
NEG = -1e30
F_PAD = 128
GATE_ROWS = 16
ATTN_HEADS_PER_STEP = 2
SCORE_BOUND_SLACK = 1.02
MAX_SAFE_SHIFT = 56.0
LOG2E = 1.4426950408889634
SUBLANES = 8
MXU_TILE = 256
F32 = jnp.float32
BF16 = jnp.bfloat16
VMEM_LIMIT_BYTES = 56 * 1024 * 1024


def _round_up(x, m):
    return -(-x // m) * m


def _full_spec(shape):
    return pl.BlockSpec(shape, lambda *_: (0,) * len(shape))


def _params(n_axes):
    return pltpu.CompilerParams(dimension_semantics=("arbitrary",) * n_axes,
                                vmem_limit_bytes=VMEM_LIMIT_BYTES)


def _dot(a, b):
    return jnp.dot(a, b, preferred_element_type=F32)


def _dot_nt(a, b):
    return lax.dot_general(a, b, (((1,), (1,)), ((), ())), preferred_element_type=F32)


def _rms(x, gain):
    y = x * lax.rsqrt(jnp.mean(x * x, axis=-1, keepdims=True) + EPS)
    return y * gain


def _gelu(x):
    return 0.5 * x * (1.0 + jnp.tanh(0.7978845608028654 * (x + 0.044715 * (x * x * x))))


def _softplus(x):
    return jnp.maximum(x, 0.0) + jnp.log1p(jnp.exp(-jnp.abs(x)))


class _Geom:
    def __init__(self, n_seq, rows_per_seq, dil, tile):
        self.n_seq, self.rows_per_seq, self.dil = n_seq, rows_per_seq, dil
        self.tm = min(tile, rows_per_seq)
        assert rows_per_seq % self.tm == 0 and self.tm % dil == 0 and self.tm % SUBLANES == 0
        self.tps = rows_per_seq // self.tm
        self.rows = n_seq * rows_per_seq
        self.n_tiles = self.rows // self.tm

    def halo(self, width):
        return _round_up((width - 1) * self.dil, SUBLANES)


def _conv_from_ext(ext_ref, w_ref, bias, width, dil, tm, hp):
    y = bias
    for j in range(width):
        off = hp - (width - 1 - j) * dil
        y = y + ext_ref[off:off + tm, :] * w_ref[j:j + 1, :]
    return y


def _ab_in_kernel(x_ref, g_ref, wqk_ref, wv_ref, wrg_ref, wf_ref, bf_ref, qg_ref, kg_ref, bd_ref,
                  cw_ref, cb_ref, wa_ref, ba_ref, wx_ref, bx_ref, lam_ref, cprev_ref, h0_ref,
                  q_out, k_out, v_out, kt_out, vt_out, lft_out, y_out, cst_out, h_out,
                  ext_ref, hc_ref, *, tm, tps, dil, hp, q_mult):
    i = pl.program_id(0)
    first = (i % tps) == 0
    xn = _rms(x_ref[...], g_ref[...]).astype(BF16)

    bd = bd_ref[...]

    def head_norm(z, gain):
        z2 = (z * z).astype(BF16)
        wide = bd.shape[0]
        ms = jnp.concatenate([_dot(z2[:, c:c + wide], bd) for c in range(0, A_WIDTH, wide)], axis=1)
        return z * lax.rsqrt(ms + EPS) * gain

    qk = _dot(xn, wqk_ref[...])
    q = head_norm(qk[:, :A_WIDTH], qg_ref[...])
    k = head_norm(qk[:, A_WIDTH:], kg_ref[...])
    v = _dot(xn, wv_ref[...])
    q = q * q_mult

    f = _dot(xn, wf_ref[...]) + bf_ref[...]
    lf = jnp.minimum(f, 0.0) - jnp.log1p(jnp.exp(-jnp.abs(f)))

    unit = tm if dil == 1 else dil
    for u in range(tm // unit):
        rows_u = slice(u * unit, (u + 1) * unit)
        kt = k[rows_u, :].T
        vt = v[rows_u, :].T
        kt_out[u] = kt
        vt_out[u] = vt
        lft_out[u] = lf[rows_u, :].T[:FOX_HEADS, :]
        if dil == 1:
            q_out[u] = q.T.astype(BF16)
            k_out[u] = kt.astype(BF16)
            v_out[u] = vt.astype(BF16)
    if dil != 1:
        q_out[...] = q.astype(BF16)
        k_out[...] = k.astype(BF16)
        v_out[...] = v.astype(BF16)

    rg = _dot(xn, wrg_ref[...])
    xr = rg[:, :B_WIDTH]
    gate = rg[:, B_WIDTH:]

    @pl.when(first)
    def _():
        ext_ref[0:hp, :] = cprev_ref[0]
        hc_ref[...] = h0_ref[0]

    ext_ref[hp:hp + tm, :] = xr
    xc = _conv_from_ext(ext_ref, cw_ref, cb_ref[...], RG_CONV, dil, tm, hp)
    tail = ext_ref[tm:tm + hp, :]
    ext_ref[0:hp, :] = tail
    cst_out[0] = tail

    xcb = xc.astype(BF16)
    r = jax.nn.sigmoid(_dot(xcb, wa_ref[...]) + ba_ref[...])
    gi = jax.nn.sigmoid(_dot(xcb, wx_ref[...]) + bx_ref[...])
    log_a = -RG_C * r * _softplus(-lam_ref[...])
    a = jnp.exp(log_a)
    inp = jnp.sqrt(-jnp.tanh(log_a) * (a * a + 1.0)) * gi * xc

    rows = lax.broadcasted_iota(jnp.int32, (tm, 1), 0)
    s = dil
    while s < tm:
        valid = rows >= s
        a_sh = jnp.where(valid, pltpu.roll(a, s, 0), 1.0)
        h_sh = jnp.where(valid, pltpu.roll(inp, s, 0), 0.0)
        inp = a * h_sh + inp
        a = a * a_sh
        s *= 2

    hc = hc_ref[0:dil, :]
    if dil == 1:
        hc_rows = jnp.broadcast_to(hc, (tm, B_WIDTH))
    else:
        hc_rows = jnp.concatenate([hc] * (tm // dil), axis=0)
    h = inp + a * hc_rows
    hc_ref[0:dil, :] = h[tm - dil:tm, :]
    h_out[0] = hc_ref[...]
    y_out[...] = (h * _gelu(gate)).astype(BF16)


def _ab_in_call(x, geom, w, cprev, h0, q_mult):
    tm, hp, tps = geom.tm, geom.halo(RG_CONV), geom.tps
    dilp = _round_up(geom.dil, SUBLANES)
    rows = geom.rows
    tile = lambda n: pl.BlockSpec((tm, n), lambda i: (i, 0))
    seq3 = lambda r, n: pl.BlockSpec((1, r, n), lambda i: (i // tps, 0, 0))
    if geom.dil == 1:
        t_shape = lambda ch: (geom.n_seq, ch, geom.rows_per_seq)
        t_spec = lambda ch: pl.BlockSpec((1, ch, tm), lambda i: (i // tps, 0, i % tps))
        op_spec, op_shape = t_spec(A_WIDTH), jax.ShapeDtypeStruct(t_shape(A_WIDTH), BF16)
    else:
        t_shape = lambda ch: (rows // geom.dil, ch, geom.dil)
        t_spec = lambda ch: pl.BlockSpec((tm // geom.dil, ch, geom.dil), lambda i: (i, 0, 0))
        op_spec, op_shape = tile(A_WIDTH), jax.ShapeDtypeStruct((rows, A_WIDTH), BF16)
    consts = [w["g"], w["wqk"], w["wv"], w["wrg"], w["wf"], w["bf"], w["qg"], w["kg"], w["bd"],
              w["cw"], w["cb"], w["wa"], w["ba"], w["wx"], w["bx"], w["lam"]]
    kern = functools.partial(_ab_in_kernel, tm=tm, tps=tps, dil=geom.dil, hp=hp, q_mult=q_mult)
    return pl.pallas_call(
        kern,
        grid=(geom.n_tiles,),
        in_specs=[tile(D_MODEL)] + [_full_spec(c.shape) for c in consts]
        + [seq3(hp, B_WIDTH), seq3(dilp, B_WIDTH)],
        out_specs=[op_spec, op_spec, op_spec,
                   t_spec(A_WIDTH), t_spec(A_WIDTH), t_spec(FOX_HEADS), tile(B_WIDTH),
                   seq3(hp, B_WIDTH), seq3(dilp, B_WIDTH)],
        out_shape=[op_shape, op_shape, op_shape,
                   jax.ShapeDtypeStruct(t_shape(A_WIDTH), F32),
                   jax.ShapeDtypeStruct(t_shape(A_WIDTH), F32),
                   jax.ShapeDtypeStruct(t_shape(FOX_HEADS), F32),
                   jax.ShapeDtypeStruct((rows, B_WIDTH), BF16),
                   jax.ShapeDtypeStruct((geom.n_seq, hp, B_WIDTH), F32),
                   jax.ShapeDtypeStruct((geom.n_seq, dilp, B_WIDTH), F32)],
        scratch_shapes=[pltpu.VMEM((hp + tm, B_WIDTH), F32), pltpu.VMEM((dilp, B_WIDTH), F32)],
        compiler_params=_params(1),
    )(x, *consts, cprev, h0)


def _cumsum_kernel(x_ref, shift_ref, cq_ref, ck_ref, *, n):
    x = x_ref[...]
    lane = lax.broadcasted_iota(jnp.int32, x.shape, 1)
    s = 1
    while s < n:
        x = x + jnp.where(lane >= s, pltpu.roll(x, s, 1), 0.0)
        s *= 2
    x = x * LOG2E

    def split(v):
        hi = v.astype(BF16).astype(F32)
        rest = v - hi
        mid = rest.astype(BF16).astype(F32)
        return hi, mid, rest - mid

    q_parts, k_parts = split(x - shift_ref[...]), split(x)
    row = lax.broadcasted_iota(jnp.int32, (GATE_ROWS, n), 0)
    for r in range(x.shape[0]):
        cq = jnp.where(row < 3, -1.0, 0.0)
        ck = jnp.where((row >= 3) & (row < 6), 1.0, 0.0)
        for j in range(3):
            cq = jnp.where(row == 3 + j, jnp.broadcast_to(q_parts[j][r:r + 1, :], (GATE_ROWS, n)), cq)
            ck = jnp.where(row == j, jnp.broadcast_to(k_parts[j][r:r + 1, :], (GATE_ROWS, n)), ck)
        cq_ref[r] = cq.astype(BF16)
        ck_ref[r] = ck.astype(BF16)


def _cumsum_call(x, shift):
    rows, n = x.shape
    out = jax.ShapeDtypeStruct((rows, GATE_ROWS, n), BF16)
    return pl.pallas_call(
        functools.partial(_cumsum_kernel, n=n),
        grid=(1,),
        in_specs=[_full_spec(x.shape), _full_spec(shift.shape)],
        out_specs=[_full_spec(out.shape)] * 2,
        out_shape=[out] * 2,
        compiler_params=_params(1),
    )(x, shift)


def _attn_prompt_kernel(bounded_ref, qt_ref, cq_ref, kt_ref, ck_ref, vt_ref, o_ref, *, tq, sb, depth):
    for online in (False, True):
        @pl.when((bounded_ref[0] == 0) == online)
        def _(online=online):
            _attn_prompt_body(qt_ref, cq_ref, kt_ref, ck_ref, vt_ref, o_ref,
                              tq=tq, sb=sb, depth=depth, online=online)


def _attn_prompt_body(qt_ref, cq_ref, kt_ref, ck_ref, vt_ref, o_ref, *, tq, sb, depth, online):
    qi = pl.program_id(2)
    n_heads = qt_ref.shape[1]
    heads = range(n_heads)
    qt = [jnp.concatenate([qt_ref[0, h], cq_ref[0, h]], axis=0) for h in heads]
    n_sb = tq // sb
    row = lax.broadcasted_iota(jnp.int32, (sb, tq), 0)
    col = lax.broadcasted_iota(jnp.int32, (sb, tq), 1)

    def key_rows(j, s):
        return pl.ds(pl.multiple_of(j * tq + s * sb, sb), sb)

    def scores(h, j, s):
        kt = jnp.concatenate([kt_ref[0, h, :, key_rows(j, s)], ck_ref[0, h, :, key_rows(j, s)]], axis=0)
        return lax.dot_general(kt, qt[h], (((0,), (0,)), ((), ())), preferred_element_type=F32)

    def chunk(j, carry, diagonal):
        state, ahead = list(carry[0]), [list(a) for a in carry[1]]
        for s in range(n_sb):
            for h in heads:
                m, l, acc = state[h]
                st = ahead[h].pop(0)
                if s + depth < n_sb:
                    ahead[h].append(scores(h, j, s + depth))
                elif not diagonal:
                    ahead[h].append(scores(h, j + 1, s + depth - n_sb))
                if diagonal:
                    st = jnp.where(row + s * sb <= col, st, NEG)
                if online:
                    m_new = jnp.maximum(m, jnp.max(st, axis=0, keepdims=True))
                    alpha = jnp.exp2(m - m_new)
                    p = jnp.exp2(st - m_new)
                    l = alpha * l + jnp.sum(p, axis=0, keepdims=True)
                    acc = alpha * acc + _dot(vt_ref[0, h, :, key_rows(j, s)], p.astype(BF16))
                    m = m_new
                else:
                    p = jnp.exp2(st)
                    l = l + jnp.sum(p, axis=0, keepdims=True)
                    acc = acc + _dot(vt_ref[0, h, :, key_rows(j, s)], p.astype(BF16))
                state[h] = (m, l, acc)
        return tuple(state), tuple(tuple(a) for a in ahead)

    state = tuple((jnp.full((1, tq), NEG, F32), jnp.zeros((1, tq), F32), jnp.zeros((HEAD_DIM, tq), F32))
                  for _ in heads)
    carry = (state, tuple(tuple(scores(h, 0, s) for s in range(depth)) for h in heads))
    carry = lax.fori_loop(0, qi, lambda j, c: chunk(j, c, False), carry)
    state, _ = chunk(qi, carry, True)
    for h, (_, l, acc) in enumerate(state):
        o_ref[0, h] = (acc / l).astype(BF16)


def _attn_prompt_call(bounded, q_t, cq, k_t, ck, v_t, tq):
    bsz, nh, dh, t = q_t.shape
    sb = min(128, tq)
    depth = min(2, tq // sb)
    hg = ATTN_HEADS_PER_STEP
    tile = lambda r: pl.BlockSpec((1, hg, r, tq), lambda b, h, i, flag: (b, h, 0, i))
    whole = lambda r: pl.BlockSpec((1, hg, r, t), lambda b, h, i, flag: (b, h, 0, 0))
    grid_spec = pltpu.PrefetchScalarGridSpec(
        num_scalar_prefetch=1, grid=(bsz, nh // hg, t // tq),
        in_specs=[tile(dh), tile(GATE_ROWS), whole(dh), whole(GATE_ROWS), whole(dh)],
        out_specs=tile(dh))
    return pl.pallas_call(
        functools.partial(_attn_prompt_kernel, tq=tq, sb=sb, depth=depth),
        grid_spec=grid_spec,
        out_shape=jax.ShapeDtypeStruct((bsz, nh, dh, t), BF16),
        compiler_params=_params(3),
    )(bounded, q_t, cq, k_t, ck, v_t)


def _attn_sample_kernel(pt_ref, q_ref, kn_ref, vn_ref, lfn_ref, *refs, n_pages, page, dt):
    del pt_ref
    k_refs = refs[:n_pages]
    v_refs = refs[n_pages:2 * n_pages]
    lf_refs = refs[2 * n_pages:3 * n_pages]
    o_ref = refs[3 * n_pages]
    nh = FOX_HEADS
    past = n_pages * page
    bdot = lambda a, b: lax.dot_general(a, b, (((2,), (1,)), ((0,), (0,))), preferred_element_type=F32)
    bdot_nt = lambda a, b: lax.dot_general(a, b, (((2,), (2,)), ((0,), (0,))), preferred_element_type=F32)

    q3 = q_ref[0]

    lft = jnp.concatenate([r[0, 0] for r in lf_refs], axis=1)
    lane = lax.broadcasted_iota(jnp.int32, lft.shape, 1)
    suf = lft
    s = 1
    while s < past:
        suf = suf + jnp.where(lane < past - s, pltpu.roll(suf, past - s, 1), 0.0)
        s *= 2
    suf = suf - lft
    suf3 = jnp.stack([jnp.broadcast_to(suf[h:h + 1, :], (dt, past)) for h in range(nh)], axis=0)

    lnew = jnp.broadcast_to(lfn_ref[0], (nh, dt, dt))
    colq = lax.broadcasted_iota(jnp.int32, (nh, dt, dt), 2)
    tq = lax.broadcasted_iota(jnp.int32, (nh, dt, dt), 1)
    causal = colq <= tq
    nq = jnp.sum(jnp.where(causal, lnew, 0.0), axis=2, keepdims=True)
    g = jnp.zeros((nh, dt, dt), F32)
    for l in range(dt):
        g = g + jnp.where(colq >= l, lnew[:, :, l:l + 1], 0.0)

    kt_all = jnp.concatenate([r[0, 0].astype(BF16) for r in k_refs], axis=2)
    vt_all = jnp.concatenate([r[0, 0].astype(BF16) for r in v_refs], axis=2)
    s_past = bdot(q3, kt_all) + (suf3 + nq)
    s_new = jnp.where(causal, bdot_nt(q3, kn_ref[0]) + (nq - g), NEG)

    m = jnp.maximum(jnp.max(s_new, axis=2, keepdims=True), jnp.max(s_past, axis=2, keepdims=True))
    p_new = jnp.exp(s_new - m)
    p_past = jnp.exp(s_past - m)
    l = jnp.sum(p_new, axis=2, keepdims=True) + jnp.sum(p_past, axis=2, keepdims=True)
    acc = bdot(p_new.astype(BF16), vn_ref[0]) + bdot_nt(p_past.astype(BF16), vt_all)
    o_ref[0] = (acc / l).astype(BF16)


def _attn_sample_call(page_table, q, k_new, v_new, lf_new, cache_kt, cache_vt, cache_lft, layer):
    db, nh, dt, dh = q.shape
    n_pages = page_table.shape[1]
    page = cache_kt.shape[-1]
    new_spec = pl.BlockSpec((1, nh, dt, dh), lambda b, pt: (b, 0, 0, 0))

    def page_spec(shape, p):
        zeros = (0,) * len(shape)
        return pl.BlockSpec((1, 1) + shape, lambda b, pt: (layer, pt[b, p]) + zeros)

    in_specs = ([new_spec, new_spec, new_spec,
                 pl.BlockSpec((1, nh, 1, dt), lambda b, pt: (b, 0, 0, 0))]
                + [page_spec((nh, dh, page), p) for p in range(n_pages)]
                + [page_spec((nh, dh, page), p) for p in range(n_pages)]
                + [page_spec((nh, page), p) for p in range(n_pages)])
    grid_spec = pltpu.PrefetchScalarGridSpec(
        num_scalar_prefetch=1, grid=(db,), in_specs=in_specs, out_specs=new_spec)
    return pl.pallas_call(
        functools.partial(_attn_sample_kernel, n_pages=n_pages, page=page, dt=dt),
        grid_spec=grid_spec,
        out_shape=jax.ShapeDtypeStruct((db, nh, dt, dh), BF16),
        compiler_params=_params(1),
    )(page_table, q, k_new, v_new, lf_new,
      *([cache_kt] * n_pages), *([cache_vt] * n_pages), *([cache_lft] * n_pages))


def _pool_kernel(x_ref, gain_ref, prev_ref, w_ref, scale_ref, o_ref, st_ref, ext_ref,
                 *, tm, tps, dil, hp, pos0):
    i = pl.program_id(0)
    first = (i % tps) == 0
    x = x_ref[...]
    xn = _rms(x, gain_ref[...])
    rows = lax.broadcasted_iota(jnp.int32, (tm, 1), 0) + (i % tps) * tm
    pos = lax.div(rows, jnp.int32(dil)) + pos0

    @pl.when(first)
    def _():
        ext_ref[0:hp, :] = prev_ref[0]

    ext_ref[hp:hp + tm, :] = xn
    for gi, window in enumerate(POOL_WINDOWS):
        cols = slice(gi * POOL_GD, (gi + 1) * POOL_GD)
        xg = xn[:, cols]
        total = xg
        for j in range(1, window):
            total = total + ext_ref[hp - j * dil:hp - j * dil + tm, cols]
        cnt = jnp.minimum(pos + 1, window).astype(F32)
        diff = total / cnt - xg
        y = _dot(diff.astype(BF16), w_ref[gi]) * scale_ref[:, cols]
        o_ref[:, cols] = x[:, cols] + y
    tail = ext_ref[tm:tm + hp, :]
    ext_ref[0:hp, :] = tail
    st_ref[0] = tail


def _pool_call(x, geom, gain, prev, w, scale, pos0):
    tm, hp = geom.tm, geom.halo(POOL_BUF + 1)
    kern = functools.partial(_pool_kernel, tm=tm, tps=geom.tps, dil=geom.dil, hp=hp, pos0=pos0)
    resident = {"pipeline_mode": pl.Buffered(1)} if geom.n_seq == 1 else {}
    return pl.pallas_call(
        kern,
        grid=(geom.n_tiles,),
        in_specs=[pl.BlockSpec((tm, D_MODEL), lambda i: (i, 0)),
                  _full_spec(gain.shape),
                  pl.BlockSpec((1, hp, D_MODEL), lambda i: (i // geom.tps, 0, 0), **resident),
                  _full_spec(w.shape),
                  _full_spec(scale.shape)],
        out_specs=[pl.BlockSpec((tm, D_MODEL), lambda i: (i, 0)),
                   pl.BlockSpec((1, hp, D_MODEL), lambda i: (i, 0, 0))],
        out_shape=[jax.ShapeDtypeStruct((geom.rows, D_MODEL), F32),
                   jax.ShapeDtypeStruct((geom.n_tiles, hp, D_MODEL), F32)],
        scratch_shapes=[pltpu.VMEM((hp + tm, D_MODEL), F32)],
        compiler_params=_params(1),
    )(x, gain, prev, w, scale)


def _ffn_kernel(x_ref, *refs, tm, tps, dil, hp, ck, mixer):
    if mixer is not None:
        a_ref, y_ref, wa_ref, wy_ref = refs[:4]
        refs = refs[4:]
    (g_ref, wug_ref, wuv_ref, cwg_ref, cwv_ref, cbg_ref, cbv_ref, pg_ref, pv_ref, wd_ref,
     o_ref, sg_ref, sv_ref, xn_ref, carg_ref, carv_ref) = refs[:16]
    ext_refs = refs[16:]
    i = pl.program_id(0)
    c = pl.program_id(1)
    first = (i % tps) == 0
    n_sub = len(ext_refs) // 2
    extg, extv = ext_refs[:n_sub], ext_refs[n_sub:]
    starts = [sum(ck[:s]) for s in range(n_sub)]
    cols = lambda s: slice(starts[s], starts[s] + ck[s])

    @pl.when(c == 0)
    def _():
        x = x_ref[...]
        if mixer == "transposed":
            x = x + lax.dot_general(a_ref[0], wa_ref[...], (((0,), (0,)), ((), ())),
                                    preferred_element_type=F32)
        elif mixer == "rows":
            x = x + _dot(a_ref[...], wa_ref[...])
        if mixer is not None:
            x = x + _dot(y_ref[...], wy_ref[...])
        xn_ref[...] = _rms(x, g_ref[...]).astype(BF16)
        o_ref[...] = x

    @pl.when(first)
    def _():
        for s in range(n_sub):
            extg[s][0:hp, :] = pg_ref[0, :, cols(s)]
            extv[s][0:hp, :] = pv_ref[0, :, cols(s)]

    @pl.when(jnp.logical_not(first))
    def _():
        for s in range(n_sub):
            extg[s][0:hp, :] = carg_ref[c, :, cols(s)]
            extv[s][0:hp, :] = carv_ref[c, :, cols(s)]

    xn = xn_ref[...]

    def up(s):
        extg[s][hp:hp + tm, :] = _dot(xn, wug_ref[:, cols(s)])
        extv[s][hp:hp + tm, :] = _dot(xn, wuv_ref[:, cols(s)])

    def gated(s):
        yg = _conv_from_ext(extg[s], cwg_ref[:, cols(s)], cbg_ref[:, cols(s)], FFN_CONV, dil, tm, hp)
        yv = _conv_from_ext(extv[s], cwv_ref[:, cols(s)], cbv_ref[:, cols(s)], FFN_CONV, dil, tm, hp)
        return (_gelu(yg) * yv).astype(BF16)

    ahead = min(2, n_sub)
    for s in range(ahead):
        up(s)
    acc = None
    for s in range(n_sub):
        if s + ahead < n_sub:
            up(s + ahead)
        down = _dot(gated(s), wd_ref[cols(s), :])
        acc = down if acc is None else acc + down
        for ext, car_ref, s_ref in ((extg[s], carg_ref, sg_ref), (extv[s], carv_ref, sv_ref)):
            tail = ext[tm:tm + hp, :]
            car_ref[c, :, cols(s)] = tail
            s_ref[0, :, cols(s)] = tail
    o_ref[...] += acc


def _ffn_call(x, geom, gain, w_up, conv_w, conv_b, prev, w_down, block, ck, mixer_out=None):
    tm, hp = geom.tm, geom.halo(FFN_CONV)
    nblk = D_FF // block
    assert sum(ck) == block
    tps = geom.tps
    resident = {"pipeline_mode": pl.Buffered(1)} if nblk == 1 else {}
    col = lambda r, off, **kw: pl.BlockSpec((r, block), lambda i, c: (0, off + c), **kw)
    st_in = lambda off: pl.BlockSpec((1, hp, block), lambda i, c: (i // tps, 0, off + c))
    st_out = pl.BlockSpec((1, hp, block), lambda i, c: (i, 0, c))
    rows_spec = lambda n: pl.BlockSpec((tm, n), lambda i, c: (i, 0))
    mixer, mixer_args, mixer_specs = None, [], []
    if mixer_out is not None:
        attn, yrg, w_attn, w_rg = mixer_out
        mixer = "transposed" if attn.ndim == 3 else "rows"
        a_spec = (pl.BlockSpec((1, A_WIDTH, tm), lambda i, c: (i // tps, 0, i % tps))
                  if mixer == "transposed" else rows_spec(A_WIDTH))
        w_spec = lambda w: pl.BlockSpec(w.shape, lambda i, c: (0, 0), pipeline_mode=pl.Buffered(1))
        mixer_args = [attn, yrg, w_attn, w_rg]
        mixer_specs = [a_spec, rows_spec(B_WIDTH), w_spec(w_attn), w_spec(w_rg)]
    kern = functools.partial(_ffn_kernel, tm=tm, tps=tps, dil=geom.dil, hp=hp, ck=ck, mixer=mixer)
    return pl.pallas_call(
        kern,
        grid=(geom.n_tiles, nblk),
        in_specs=[rows_spec(D_MODEL)] + mixer_specs + [
                  _full_spec(gain.shape),
                  col(D_MODEL, 0, **resident), col(D_MODEL, nblk, **resident),
                  col(FFN_CONV, 0), col(FFN_CONV, nblk),
                  col(1, 0), col(1, nblk),
                  st_in(0), st_in(nblk),
                  pl.BlockSpec((block, D_MODEL), lambda i, c: (c, 0), **resident)],
        out_specs=[pl.BlockSpec((tm, D_MODEL), lambda i, c: (i, 0)), st_out, st_out],
        out_shape=[jax.ShapeDtypeStruct((geom.rows, D_MODEL), F32),
                   jax.ShapeDtypeStruct((geom.n_tiles, hp, D_FF), F32),
                   jax.ShapeDtypeStruct((geom.n_tiles, hp, D_FF), F32)],
        scratch_shapes=[pltpu.VMEM((tm, D_MODEL), BF16),
                        pltpu.VMEM((nblk, hp, block), F32), pltpu.VMEM((nblk, hp, block), F32)]
        + [pltpu.VMEM((hp + tm, w), F32) for w in ck] * 2,
        compiler_params=_params(2),
    )(x, *mixer_args, gain, w_up, w_up, conv_w, conv_w, conv_b, conv_b, prev, prev, w_down)


def _state_to_halo(state, geom, hp):
    n, w1, ch = state.shape
    if geom.dil == 1:
        rows = state
    else:
        rows = state.transpose(1, 0, 2).reshape(1, w1 * n, ch)
    return jnp.pad(rows, ((0, 0), (hp - rows.shape[1], 0), (0, 0)))


def _halo_to_state(halo, geom, n, w1):
    ch = halo.shape[-1]
    if halo.shape[0] != geom.n_seq:
        halo = halo[geom.tps - 1::geom.tps]
    if geom.dil == 1:
        return halo[:, halo.shape[1] - w1:, :]
    return halo[0, halo.shape[1] - w1 * n:, :].reshape(w1, n, ch).transpose(1, 0, 2)


def _block_diag(w):
    nb, bi, bj = w.shape
    eye = jnp.eye(nb, dtype=w.dtype)
    return (eye[:, None, :, None] * w[:, :, None, :]).reshape(nb * bi, nb * bj)


def _trunk(x_rows, geom, n, pos0, p, paged, rg_h, rg_conv, pool_buf, ffn_buf, ffn_ck):
    depth = p["norm_mix"].shape[0]
    ks, vs, lfs, hs, cs, pbs, fbs = [], [], [], [], [], [], []
    dil = geom.dil
    t_steps = geom.rows // n
    dilp = _round_up(dil, SUBLANES)
    x = x_rows
    for layer in range(depth):
        li = layer // 2
        if layer % 2 == 0:
            w_in = p["ab_w_in"][li]
            a3 = 3 * A_WIDTH
            w = {
                "g": p["norm_mix"][layer][None, :],
                "wqk": w_in[:, :2 * A_WIDTH].astype(BF16),
                "wv": w_in[:, 2 * A_WIDTH:a3].astype(BF16),
                "wf": jnp.pad(w_in[:, a3:a3 + FOX_HEADS], ((0, 0), (0, F_PAD - FOX_HEADS))).astype(BF16),
                "wrg": w_in[:, a3 + FOX_HEADS:].astype(BF16),
                "bf": jnp.pad(p["ab_b_f"][li], (0, F_PAD - FOX_HEADS))[None, :],
                "qg": jnp.tile(p["ab_q_gain"][li], FOX_HEADS)[None, :],
                "kg": jnp.tile(p["ab_k_gain"][li], FOX_HEADS)[None, :],
                "bd": _block_diag(jnp.full((MXU_TILE // HEAD_DIM, HEAD_DIM, HEAD_DIM), 1.0 / HEAD_DIM,
                                           F32)).astype(BF16),
                "cw": p["ab_conv_w"][li], "cb": p["ab_conv_b"][li][None, :],
                "wa": _block_diag(p["ab_w_a"][li]).astype(BF16), "ba": p["ab_b_a"][li][None, :],
                "wx": _block_diag(p["ab_w_x"][li]).astype(BF16), "bx": p["ab_b_x"][li][None, :],
                "lam": p["ab_lambda"][li][None, :],
            }
            hp = geom.halo(RG_CONV)
            cprev = _state_to_halo(rg_conv[li], geom, hp)
            if dil == 1:
                h0 = jnp.pad(rg_h[li][:, None, :], ((0, 0), (0, dilp - 1), (0, 0)))
            else:
                h0 = jnp.pad(rg_h[li][None], ((0, 0), (0, dilp - dil), (0, 0)))
            scale = HEAD_DIM ** -0.5
            q_mult = scale * LOG2E if paged is None else scale
            q, k, v, kt, vt, lft, yrg, cst, hl = _ab_in_call(x, geom, w, cprev, h0, q_mult)

            if paged is None:
                shift = (SCORE_BOUND_SLACK * HEAD_DIM * q_mult * jnp.max(jnp.abs(p["ab_q_gain"][li]))
                         * jnp.max(jnp.abs(p["ab_k_gain"][li])) + 0.5)
                bounded = (shift < MAX_SAFE_SHIFT).astype(jnp.int32).reshape(1)
                cq, ck = _cumsum_call(lft.reshape(n * FOX_HEADS, t_steps), shift.reshape(1, 1))
                cq = cq.reshape(n, FOX_HEADS, GATE_ROWS, t_steps)
                ck = ck.reshape(n, FOX_HEADS, GATE_ROWS, t_steps)
                heads_t = lambda z: z.reshape(n, FOX_HEADS, HEAD_DIM, t_steps)
                attn = _attn_prompt_call(bounded, heads_t(q), cq, heads_t(k), ck, heads_t(v),
                                         min(512, t_steps))
                attn = attn.reshape(n, A_WIDTH, t_steps)
                ks.append(kt.reshape(n, FOX_HEADS, HEAD_DIM, t_steps).transpose(0, 3, 1, 2))
                vs.append(vt.reshape(n, FOX_HEADS, HEAD_DIM, t_steps).transpose(0, 3, 1, 2))
                lfs.append(lft.transpose(0, 2, 1))
                hs.append(hl[:, 0, :])
            else:
                ckt, cvt, clft, page_table = paged

                def heads_bm(z):
                    return z.reshape(t_steps, n, FOX_HEADS, HEAD_DIM).transpose(1, 2, 0, 3)
                lf_new = lft.transpose(2, 1, 0)[:, :, None, :]
                attn = _attn_sample_call(page_table, heads_bm(q), heads_bm(k), heads_bm(v), lf_new,
                                         ckt, cvt, clft, li)
                attn = attn.transpose(2, 0, 1, 3).reshape(geom.rows, A_WIDTH)
                ks.append(kt.reshape(t_steps, FOX_HEADS, HEAD_DIM, n).transpose(3, 0, 1, 2))
                vs.append(vt.reshape(t_steps, FOX_HEADS, HEAD_DIM, n).transpose(3, 0, 1, 2))
                lfs.append(lft.transpose(2, 0, 1))
                hs.append(hl[0, :dil, :])
            cs.append(_halo_to_state(cst, geom, n, RG_CONV - 1))
            w_out = p["ab_w_out"][li].astype(BF16)
            mixer_out = (attn, yrg, w_out[:A_WIDTH], w_out[A_WIDTH:])
        else:
            mixer_out = None
            hp = geom.halo(POOL_BUF + 1)
            prev = _state_to_halo(pool_buf[li], geom, hp)
            x, st = _pool_call(x, geom, p["norm_mix"][layer][None, :], prev,
                               p["pool_w"][li].astype(BF16), p["pool_scale"][li][None, :], pos0)
            pbs.append(_halo_to_state(st, geom, n, POOL_BUF))
        hp = geom.halo(FFN_CONV)
        prev = _state_to_halo(ffn_buf[layer], geom, hp)
        x, sg, sv = _ffn_call(x, geom, p["norm_ffn"][layer][None, :], p["ffn_w_up"][layer].astype(BF16),
                              p["ffn_conv_w"][layer], p["ffn_conv_b"][layer][None, :], prev,
                              p["ffn_w_down"][layer].astype(BF16), *ffn_ck, mixer_out=mixer_out)
        fbs.append(_halo_to_state(jnp.concatenate([sg, sv], axis=-1), geom, n, FFN_CONV - 1))
    return x, (jnp.stack(ks), jnp.stack(vs), jnp.stack(lfs), jnp.stack(hs), jnp.stack(cs),
               jnp.stack(pbs), jnp.stack(fbs))


def kernel(x_prompt, x_sample, cache_k, cache_v, cache_logf, state_rg_h, state_rg_conv, state_pool, state_ffn_conv, page_table, norm_mix, norm_ffn, ab_w_in, ab_b_f, ab_q_gain, ab_k_gain, ab_conv_w, ab_conv_b, ab_w_a, ab_b_a, ab_w_x, ab_b_x, ab_lambda, ab_w_out, pool_w, pool_scale, ffn_w_up, ffn_conv_w, ffn_conv_b, ffn_w_down):
    p = {
        "norm_mix": norm_mix, "norm_ffn": norm_ffn,
        "ab_w_in": ab_w_in, "ab_b_f": ab_b_f, "ab_q_gain": ab_q_gain, "ab_k_gain": ab_k_gain,
        "ab_conv_w": ab_conv_w, "ab_conv_b": ab_conv_b, "ab_w_a": ab_w_a, "ab_b_a": ab_b_a,
        "ab_w_x": ab_w_x, "ab_b_x": ab_b_x, "ab_lambda": ab_lambda, "ab_w_out": ab_w_out,
        "pool_w": pool_w, "pool_scale": pool_scale,
        "ffn_w_up": ffn_w_up, "ffn_conv_w": ffn_conv_w, "ffn_conv_b": ffn_conv_b, "ffn_w_down": ffn_w_down,
    }
    depth = norm_mix.shape[0]
    n_ab, n_pool = (depth + 1) // 2, depth // 2

    bsz, t, _ = x_prompt.shape
    geom_p = _Geom(bsz, t, 1, 512)
    y_p, st_p = _trunk(
        x_prompt.reshape(bsz * t, D_MODEL), geom_p, bsz, 0, p, None,
        jnp.zeros((n_ab, bsz, B_WIDTH), F32), jnp.zeros((n_ab, bsz, RG_CONV - 1, B_WIDTH), F32),
        jnp.zeros((n_pool, bsz, POOL_BUF, D_MODEL), F32),
        jnp.zeros((depth, bsz, FFN_CONV - 1, 2 * D_FF), F32), (D_FF, (1024, 1024, 1024)))
    y_prompt = y_p.reshape(bsz, t, D_MODEL)

    db, dt, _ = x_sample.shape
    n_pool_pages, page = cache_k.shape[1], cache_k.shape[2]
    past_len = page_table.shape[1] * page
    geom_s = _Geom(1, dt * db, db, 512)
    paged = (cache_k.transpose(0, 1, 3, 4, 2), cache_v.transpose(0, 1, 3, 4, 2),
             cache_logf.transpose(0, 1, 3, 2), page_table)
    y_s, st_s = _trunk(
        x_sample.transpose(1, 0, 2).reshape(dt * db, D_MODEL), geom_s, db, past_len, p, paged,
        state_rg_h, state_rg_conv, state_pool, state_ffn_conv, (512, (512,)))
    y_sample = y_s.reshape(dt, db, D_MODEL).transpose(1, 0, 2)
    return (y_prompt, y_sample) + st_p + st_s
```

```python
import functools

import jax
import jax.numpy as jnp
from jax import lax
from jax.experimental import pallas as pl
from jax.experimental.pallas import tpu as pltpu

D_MODEL = 1024
A_WIDTH = 512
B_WIDTH = 512
HEAD_DIM = 64
FOX_HEADS = 8
RG_CONV = 4
RG_C = 8.0
POOL_WINDOWS = (2, 4, 8, 16)
POOL_GD = 256
POOL_BUF = 15
D_FF = 3072
FFN_CONV = 3
EPS = 1e-6
NEG = -1e30
F_PAD = 128
GATE_ROWS = 16
ATTN_HEADS_PER_STEP = 2
SCORE_BOUND_SLACK = 1.02
AB_IN_TILE = 512
MAX_SAFE_SHIFT = 56.0
LOG2E = 1.4426950408889634
SUBLANES = 8
MXU_TILE = 256
F32 = jnp.float32
BF16 = jnp.bfloat16
VMEM_LIMIT_BYTES = 56 * 1024 * 1024
FFN_PIPELINED_VMEM_LIMIT_BYTES = 62 * 1024 * 1024


def _round_up(x, m):
    return -(-x // m) * m


def _full_spec(shape):
    return pl.BlockSpec(shape, lambda *_: (0,) * len(shape))


def _params(n_axes):
    return pltpu.CompilerParams(dimension_semantics=("arbitrary",) * n_axes,
                                vmem_limit_bytes=VMEM_LIMIT_BYTES)


def _dot(a, b):
    return jnp.dot(a, b, preferred_element_type=F32)


def _dot_nt(a, b):
    return lax.dot_general(a, b, (((1,), (1,)), ((), ())), preferred_element_type=F32)


def _rms(x, gain):
    y = x * lax.rsqrt(jnp.mean(x * x, axis=-1, keepdims=True) + EPS)
    return y * gain


def _gelu(x):
    return 0.5 * x * (1.0 + jnp.tanh(0.7978845608028654 * (x + 0.044715 * (x * x * x))))


def _gelu_doubled(x):
    c = 0.7978845608028654
    return x * (1.0 + jnp.tanh(x * (c + (c * 0.044715) * (x * x))))


def _softplus(x):
    return jnp.maximum(x, 0.0) + jnp.log1p(jnp.exp(-jnp.abs(x)))


class _Geom:
    def __init__(self, n_seq, rows_per_seq, dil, tile):
        self.n_seq, self.rows_per_seq, self.dil = n_seq, rows_per_seq, dil
        self.tm = min(tile, rows_per_seq)
        assert rows_per_seq % self.tm == 0 and self.tm % dil == 0 and self.tm % SUBLANES == 0
        self.tps = rows_per_seq // self.tm
        self.rows = n_seq * rows_per_seq
        self.n_tiles = self.rows // self.tm

    def halo(self, width):
        return _round_up((width - 1) * self.dil, SUBLANES)

    def with_tile(self, tile):
        return _Geom(self.n_seq, self.rows_per_seq, self.dil, tile)


def _conv_from_ext(ext_ref, w_ref, bias, width, dil, tm, hp):
    y = bias
    for j in range(width):
        off = hp - (width - 1 - j) * dil
        y = y + ext_ref[off:off + tm, :] * w_ref[j:j + 1, :]
    return y


def _ab_in_kernel(x_ref, g_ref, wqk_ref, wv_ref, wrg_ref, wf_ref, bf_ref, qg_ref, kg_ref, bd_ref,
                  cw_ref, cb_ref, wa_ref, ba_ref, wx_ref, bx_ref, lam_ref, cprev_ref, h0_ref,
                  q_out, k_out, v_out, kt_out, vt_out, lft_out, y_out, cst_out, h_out,
                  ext_ref, hc_ref, *, tm, tps, dil, hp, q_mult):
    i = pl.program_id(0)
    first = (i % tps) == 0
    xn = _rms(x_ref[...], g_ref[...]).astype(BF16)

    bd = bd_ref[...]

    def head_norm(z, gain):
        z2 = (z * z).astype(BF16)
        wide = bd.shape[0]
        ms = jnp.concatenate([_dot(z2[:, c:c + wide], bd) for c in range(0, A_WIDTH, wide)], axis=1)
        return z * lax.rsqrt(ms + EPS) * gain

    qk = _dot(xn, wqk_ref[...])
    q = head_norm(qk[:, :A_WIDTH], qg_ref[...])
    k = head_norm(qk[:, A_WIDTH:], kg_ref[...])
    v = _dot(xn, wv_ref[...])
    q = q * q_mult

    f = _dot(xn, wf_ref[...]) + bf_ref[...]
    lf = jnp.minimum(f, 0.0) - jnp.log1p(jnp.exp(-jnp.abs(f)))

    unit = tm if dil == 1 else dil
    for u in range(tm // unit):
        rows_u = slice(u * unit, (u + 1) * unit)
        kt = k[rows_u, :].T
        vt = v[rows_u, :].T
        kt_out[u] = kt
        vt_out[u] = vt
        lft_out[u] = lf[rows_u, :].T[:FOX_HEADS, :]
        if dil == 1:
            q_out[u] = q.T.astype(BF16)
            k_out[u] = kt.astype(BF16)
            v_out[u] = vt.astype(BF16)
    if dil != 1:
        q_out[...] = q.astype(BF16)
        k_out[...] = k.astype(BF16)
        v_out[...] = v.astype(BF16)

    rg = _dot(xn, wrg_ref[...])
    xr = rg[:, :B_WIDTH]
    gate = rg[:, B_WIDTH:]

    @pl.when(first)
    def _():
        ext_ref[0:hp, :] = cprev_ref[0]
        hc_ref[...] = h0_ref[0]

    ext_ref[hp:hp + tm, :] = xr
    xc = _conv_from_ext(ext_ref, cw_ref, cb_ref[...], RG_CONV, dil, tm, hp)
    tail = ext_ref[tm:tm + hp, :]
    ext_ref[0:hp, :] = tail
    cst_out[0] = tail

    xcb = xc.astype(BF16)
    r = jax.nn.sigmoid(_dot(xcb, wa_ref[...]) + ba_ref[...])
    gi = jax.nn.sigmoid(_dot(xcb, wx_ref[...]) + bx_ref[...])
    log_a = -RG_C * r * _softplus(-lam_ref[...])
    a = jnp.exp(log_a)
    inp = jnp.sqrt(-jnp.tanh(log_a) * (a * a + 1.0)) * gi * xc

    rows = lax.broadcasted_iota(jnp.int32, (tm, 1), 0)
    s = dil
    while s < tm:
        valid = rows >= s
        a_sh = jnp.where(valid, pltpu.roll(a, s, 0), 1.0)
        h_sh = jnp.where(valid, pltpu.roll(inp, s, 0), 0.0)
        inp = a * h_sh + inp
        a = a * a_sh
        s *= 2

    hc = hc_ref[0:dil, :]
    if dil == 1:
        hc_rows = jnp.broadcast_to(hc, (tm, B_WIDTH))
    else:
        hc_rows = jnp.concatenate([hc] * (tm // dil), axis=0)
    h = inp + a * hc_rows
    hc_ref[0:dil, :] = h[tm - dil:tm, :]
    h_out[0] = hc_ref[...]
    y_out[...] = (h * _gelu(gate)).astype(BF16)


def _ab_in_call(x, geom, w, cprev, h0, q_mult):
    tm, hp, tps = geom.tm, geom.halo(RG_CONV), geom.tps
    dilp = _round_up(geom.dil, SUBLANES)
    rows = geom.rows
    tile = lambda n: pl.BlockSpec((tm, n), lambda i: (i, 0))
    seq3 = lambda r, n: pl.BlockSpec((1, r, n), lambda i: (i // tps, 0, 0))
    if geom.dil == 1:
        t_shape = lambda ch: (geom.n_seq, ch, geom.rows_per_seq)
        t_spec = lambda ch: pl.BlockSpec((1, ch, tm), lambda i: (i // tps, 0, i % tps))
        op_spec, op_shape = t_spec(A_WIDTH), jax.ShapeDtypeStruct(t_shape(A_WIDTH), BF16)
    else:
        t_shape = lambda ch: (rows // geom.dil, ch, geom.dil)
        t_spec = lambda ch: pl.BlockSpec((tm // geom.dil, ch, geom.dil), lambda i: (i, 0, 0))
        op_spec, op_shape = tile(A_WIDTH), jax.ShapeDtypeStruct((rows, A_WIDTH), BF16)
    consts = [w["g"], w["wqk"], w["wv"], w["wrg"], w["wf"], w["bf"], w["qg"], w["kg"], w["bd"],
              w["cw"], w["cb"], w["wa"], w["ba"], w["wx"], w["bx"], w["lam"]]
    kern = functools.partial(_ab_in_kernel, tm=tm, tps=tps, dil=geom.dil, hp=hp, q_mult=q_mult)
    return pl.pallas_call(
        kern,
        grid=(geom.n_tiles,),
        in_specs=[tile(D_MODEL)] + [_full_spec(c.shape) for c in consts]
        + [seq3(hp, B_WIDTH), seq3(dilp, B_WIDTH)],
        out_specs=[op_spec, op_spec, op_spec,
                   t_spec(A_WIDTH), t_spec(A_WIDTH), t_spec(FOX_HEADS), tile(B_WIDTH),
                   seq3(hp, B_WIDTH), seq3(dilp, B_WIDTH)],
        out_shape=[op_shape, op_shape, op_shape,
                   jax.ShapeDtypeStruct(t_shape(A_WIDTH), F32),
                   jax.ShapeDtypeStruct(t_shape(A_WIDTH), F32),
                   jax.ShapeDtypeStruct(t_shape(FOX_HEADS), F32),
                   jax.ShapeDtypeStruct((rows, B_WIDTH), BF16),
                   jax.ShapeDtypeStruct((geom.n_seq, hp, B_WIDTH), F32),
                   jax.ShapeDtypeStruct((geom.n_seq, dilp, B_WIDTH), F32)],
        scratch_shapes=[pltpu.VMEM((hp + tm, B_WIDTH), F32), pltpu.VMEM((dilp, B_WIDTH), F32)],
        compiler_params=_params(1),
    )(x, *consts, cprev, h0)


def _cumsum_kernel(x_ref, shift_ref, cq_ref, ck_ref, *, n):
    x = x_ref[...]
    lane = lax.broadcasted_iota(jnp.int32, x.shape, 1)
    s = 1
    while s < n:
        x = x + jnp.where(lane >= s, pltpu.roll(x, s, 1), 0.0)
        s *= 2
    x = x * LOG2E

    def split(v):
        hi = v.astype(BF16).astype(F32)
        rest = v - hi
        mid = rest.astype(BF16).astype(F32)
        return hi, mid, rest - mid

    q_parts, k_parts = split(x - shift_ref[...]), split(x)
    row = lax.broadcasted_iota(jnp.int32, (GATE_ROWS, n), 0)
    for r in range(x.shape[0]):
        cq = jnp.where(row < 3, -1.0, 0.0)
        ck = jnp.where((row >= 3) & (row < 6), 1.0, 0.0)
        for j in range(3):
            cq = jnp.where(row == 3 + j, jnp.broadcast_to(q_parts[j][r:r + 1, :], (GATE_ROWS, n)), cq)
            ck = jnp.where(row == j, jnp.broadcast_to(k_parts[j][r:r + 1, :], (GATE_ROWS, n)), ck)
        cq_ref[r] = cq.astype(BF16)
        ck_ref[r] = ck.astype(BF16)


def _cumsum_call(x, shift):
    rows, n = x.shape
    out = jax.ShapeDtypeStruct((rows, GATE_ROWS, n), BF16)
    return pl.pallas_call(
        functools.partial(_cumsum_kernel, n=n),
        grid=(1,),
        in_specs=[_full_spec(x.shape), _full_spec(shift.shape)],
        out_specs=[_full_spec(out.shape)] * 2,
        out_shape=[out] * 2,
        compiler_params=_params(1),
    )(x, shift)


def _attn_prompt_kernel(bounded_ref, qt_ref, cq_ref, kt_ref, ck_ref, vt_ref, o_ref, *, tq, sb, depth):
    for online in (False, True):
        @pl.when((bounded_ref[0] == 0) == online)
        def _(online=online):
            _attn_prompt_body(qt_ref, cq_ref, kt_ref, ck_ref, vt_ref, o_ref,
                              tq=tq, sb=sb, depth=depth, online=online)


def _attn_prompt_body(qt_ref, cq_ref, kt_ref, ck_ref, vt_ref, o_ref, *, tq, sb, depth, online):
    qi = pl.program_id(2)
    n_heads = qt_ref.shape[1]
    heads = range(n_heads)
    qt = [jnp.concatenate([qt_ref[0, h], cq_ref[0, h]], axis=0) for h in heads]
    n_sb = tq // sb
    row = lax.broadcasted_iota(jnp.int32, (sb, tq), 0)
    col = lax.broadcasted_iota(jnp.int32, (sb, tq), 1)

    def key_rows(j, s):
        return pl.ds(pl.multiple_of(j * tq + s * sb, sb), sb)

    def scores(h, j, s):
        kt = jnp.concatenate([kt_ref[0, h, :, key_rows(j, s)], ck_ref[0, h, :, key_rows(j, s)]], axis=0)
        return lax.dot_general(kt, qt[h], (((0,), (0,)), ((), ())), preferred_element_type=F32)

    def chunk(j, carry, diagonal):
        state, ahead = list(carry[0]), [list(a) for a in carry[1]]
        for s in range(n_sb):
            for h in heads:
                m, l, acc = state[h]
                st = ahead[h].pop(0)
                if s + depth < n_sb:
                    ahead[h].append(scores(h, j, s + depth))
                elif not diagonal:
                    ahead[h].append(scores(h, j + 1, s + depth - n_sb))
                if diagonal:
                    st = jnp.where(row + s * sb <= col, st, NEG)
                if online:
                    m_new = jnp.maximum(m, jnp.max(st, axis=0, keepdims=True))
                    alpha = jnp.exp2(m - m_new)
                    p = jnp.exp2(st - m_new)
                    l = alpha * l + jnp.sum(p, axis=0, keepdims=True)
                    acc = alpha * acc + _dot(vt_ref[0, h, :, key_rows(j, s)], p.astype(BF16))
                    m = m_new
                else:
                    p = jnp.exp2(st)
                    l = l + jnp.sum(p, axis=0, keepdims=True)
                    acc = acc + _dot(vt_ref[0, h, :, key_rows(j, s)], p.astype(BF16))
                state[h] = (m, l, acc)
        return tuple(state), tuple(tuple(a) for a in ahead)

    state = tuple((jnp.full((1, tq), NEG, F32), jnp.zeros((1, tq), F32), jnp.zeros((HEAD_DIM, tq), F32))
                  for _ in heads)
    carry = (state, tuple(tuple(scores(h, 0, s) for s in range(depth)) for h in heads))
    carry = lax.fori_loop(0, qi, lambda j, c: chunk(j, c, False), carry)
    state, _ = chunk(qi, carry, True)
    for h, (_, l, acc) in enumerate(state):
        o_ref[0, h] = (acc / l).astype(BF16)


def _attn_prompt_call(bounded, q_t, cq, k_t, ck, v_t, tq):
    bsz, nh, dh, t = q_t.shape
    sb = min(128, tq)
    depth = min(2, tq // sb)
    hg = ATTN_HEADS_PER_STEP
    tile = lambda r: pl.BlockSpec((1, hg, r, tq), lambda b, h, i, flag: (b, h, 0, i))
    whole = lambda r: pl.BlockSpec((1, hg, r, t), lambda b, h, i, flag: (b, h, 0, 0))
    grid_spec = pltpu.PrefetchScalarGridSpec(
        num_scalar_prefetch=1, grid=(bsz, nh // hg, t // tq),
        in_specs=[tile(dh), tile(GATE_ROWS), whole(dh), whole(GATE_ROWS), whole(dh)],
        out_specs=tile(dh))
    return pl.pallas_call(
        functools.partial(_attn_prompt_kernel, tq=tq, sb=sb, depth=depth),
        grid_spec=grid_spec,
        out_shape=jax.ShapeDtypeStruct((bsz, nh, dh, t), BF16),
        compiler_params=_params(3),
    )(bounded, q_t, cq, k_t, ck, v_t)


def _attn_sample_kernel(pt_ref, q_ref, kn_ref, vn_ref, lfn_ref, *refs, n_pages, page, dt):
    del pt_ref
    k_refs = refs[:n_pages]
    v_refs = refs[n_pages:2 * n_pages]
    lf_refs = refs[2 * n_pages:3 * n_pages]
    o_ref = refs[3 * n_pages]
    nh = FOX_HEADS
    past = n_pages * page
    bdot = lambda a, b: lax.dot_general(a, b, (((2,), (1,)), ((0,), (0,))), preferred_element_type=F32)
    bdot_nt = lambda a, b: lax.dot_general(a, b, (((2,), (2,)), ((0,), (0,))), preferred_element_type=F32)

    q3 = q_ref[0]

    lft = jnp.concatenate([r[0, 0] for r in lf_refs], axis=1)
    lane = lax.broadcasted_iota(jnp.int32, lft.shape, 1)
    suf = lft
    s = 1
    while s < past:
        suf = suf + jnp.where(lane < past - s, pltpu.roll(suf, past - s, 1), 0.0)
        s *= 2
    suf = suf - lft
    suf3 = jnp.stack([jnp.broadcast_to(suf[h:h + 1, :], (dt, past)) for h in range(nh)], axis=0)

    lnew = jnp.broadcast_to(lfn_ref[0], (nh, dt, dt))
    colq = lax.broadcasted_iota(jnp.int32, (nh, dt, dt), 2)
    tq = lax.broadcasted_iota(jnp.int32, (nh, dt, dt), 1)
    causal = colq <= tq
    nq = jnp.sum(jnp.where(causal, lnew, 0.0), axis=2, keepdims=True)
    g = jnp.zeros((nh, dt, dt), F32)
    for l in range(dt):
        g = g + jnp.where(colq >= l, lnew[:, :, l:l + 1], 0.0)

    kt_all = jnp.concatenate([r[0, 0].astype(BF16) for r in k_refs], axis=2)
    vt_all = jnp.concatenate([r[0, 0].astype(BF16) for r in v_refs], axis=2)
    s_past = bdot(q3, kt_all) + (suf3 + nq)
    s_new = jnp.where(causal, bdot_nt(q3, kn_ref[0]) + (nq - g), NEG)

    m = jnp.maximum(jnp.max(s_new, axis=2, keepdims=True), jnp.max(s_past, axis=2, keepdims=True))
    p_new = jnp.exp(s_new - m)
    p_past = jnp.exp(s_past - m)
    l = jnp.sum(p_new, axis=2, keepdims=True) + jnp.sum(p_past, axis=2, keepdims=True)
    acc = bdot(p_new.astype(BF16), vn_ref[0]) + bdot_nt(p_past.astype(BF16), vt_all)
    o_ref[0] = (acc / l).astype(BF16)


def _attn_sample_call(page_table, q, k_new, v_new, lf_new, cache_kt, cache_vt, cache_lft, layer):
    db, nh, dt, dh = q.shape
    n_pages = page_table.shape[1]
    page = cache_kt.shape[-1]
    new_spec = pl.BlockSpec((1, nh, dt, dh), lambda b, pt: (b, 0, 0, 0))

    def page_spec(shape, p):
        zeros = (0,) * len(shape)
        return pl.BlockSpec((1, 1) + shape, lambda b, pt: (layer, pt[b, p]) + zeros)

    in_specs = ([new_spec, new_spec, new_spec,
                 pl.BlockSpec((1, nh, 1, dt), lambda b, pt: (b, 0, 0, 0))]
                + [page_spec((nh, dh, page), p) for p in range(n_pages)]
                + [page_spec((nh, dh, page), p) for p in range(n_pages)]
                + [page_spec((nh, page), p) for p in range(n_pages)])
    grid_spec = pltpu.PrefetchScalarGridSpec(
        num_scalar_prefetch=1, grid=(db,), in_specs=in_specs, out_specs=new_spec)
    return pl.pallas_call(
        functools.partial(_attn_sample_kernel, n_pages=n_pages, page=page, dt=dt),
        grid_spec=grid_spec,
        out_shape=jax.ShapeDtypeStruct((db, nh, dt, dh), BF16),
        compiler_params=_params(1),
    )(page_table, q, k_new, v_new, lf_new,
      *([cache_kt] * n_pages), *([cache_vt] * n_pages), *([cache_lft] * n_pages))


def _pool_kernel(x_ref, gain_ref, prev_ref, w_ref, scale_ref, o_ref, st_ref, ext_ref,
                 *, tm, tps, dil, hp, pos0):
    i = pl.program_id(0)
    first = (i % tps) == 0
    x = x_ref[...]
    xn = _rms(x, gain_ref[...])
    rows = lax.broadcasted_iota(jnp.int32, (tm, 1), 0) + (i % tps) * tm
    pos = lax.div(rows, jnp.int32(dil)) + pos0

    @pl.when(first)
    def _():
        ext_ref[0:hp, :] = prev_ref[0]

    ext_ref[hp:hp + tm, :] = xn
    for gi, window in enumerate(POOL_WINDOWS):
        cols = slice(gi * POOL_GD, (gi + 1) * POOL_GD)
        xg = xn[:, cols]
        total = xg
        for j in range(1, window):
            total = total + ext_ref[hp - j * dil:hp - j * dil + tm, cols]
        cnt = jnp.minimum(pos + 1, window).astype(F32)
        diff = total / cnt - xg
        y = _dot(diff.astype(BF16), w_ref[gi]) * scale_ref[:, cols]
        o_ref[:, cols] = x[:, cols] + y
    tail = ext_ref[tm:tm + hp, :]
    ext_ref[0:hp, :] = tail
    st_ref[0] = tail


def _pool_call(x, geom, gain, prev, w, scale, pos0):
    tm, hp = geom.tm, geom.halo(POOL_BUF + 1)
    kern = functools.partial(_pool_kernel, tm=tm, tps=geom.tps, dil=geom.dil, hp=hp, pos0=pos0)
    resident = {"pipeline_mode": pl.Buffered(1)} if geom.n_seq == 1 else {}
    return pl.pallas_call(
        kern,
        grid=(geom.n_tiles,),
        in_specs=[pl.BlockSpec((tm, D_MODEL), lambda i: (i, 0)),
                  _full_spec(gain.shape),
                  pl.BlockSpec((1, hp, D_MODEL), lambda i: (i // geom.tps, 0, 0), **resident),
                  _full_spec(w.shape),
                  _full_spec(scale.shape)],
        out_specs=[pl.BlockSpec((tm, D_MODEL), lambda i: (i, 0)),
                   pl.BlockSpec((1, hp, D_MODEL), lambda i: (i, 0, 0))],
        out_shape=[jax.ShapeDtypeStruct((geom.rows, D_MODEL), F32),
                   jax.ShapeDtypeStruct((geom.n_tiles, hp, D_MODEL), F32)],
        scratch_shapes=[pltpu.VMEM((hp + tm, D_MODEL), F32)],
        compiler_params=_params(1),
    )(x, gain, prev, w, scale)


def _ffn_kernel(x_ref, *refs, tm, tps, dil, hp, ck, mixer):
    if mixer is not None:
        a_ref, y_ref, wa_ref, wy_ref = refs[:4]
        refs = refs[4:]
    (g_ref, wug_ref, wuv_ref, cwg_ref, cwv_ref, cbg_ref, cbv_ref, pg_ref, pv_ref, wd_ref,
     o_ref, sg_ref, sv_ref, xn_ref, carg_ref, carv_ref) = refs[:16]
    ext_refs = refs[16:]
    i = pl.program_id(0)
    c = pl.program_id(1)
    first = (i % tps) == 0
    n_sub = len(ext_refs) // 2
    extg, extv = ext_refs[:n_sub], ext_refs[n_sub:]
    starts = [sum(ck[:s]) for s in range(n_sub)]
    cols = lambda s: slice(starts[s], starts[s] + ck[s])

    @pl.when(c == 0)
    def _():
        x = x_ref[...]
        if mixer == "transposed":
            x = x + lax.dot_general(a_ref[0], wa_ref[...], (((0,), (0,)), ((), ())),
                                    preferred_element_type=F32)
        elif mixer == "rows":
            x = x + _dot(a_ref[...], wa_ref[...])
        if mixer is not None:
            x = x + _dot(y_ref[...], wy_ref[...])
        xn_ref[...] = _rms(x, g_ref[...]).astype(BF16)
        o_ref[...] = x

    @pl.when(first)
    def _():
        for s in range(n_sub):
            extg[s][0:hp, :] = pg_ref[0, :, cols(s)]
            extv[s][0:hp, :] = pv_ref[0, :, cols(s)]

    @pl.when(jnp.logical_not(first))
    def _():
        for s in range(n_sub):
            extg[s][0:hp, :] = carg_ref[c, :, cols(s)]
            extv[s][0:hp, :] = carv_ref[c, :, cols(s)]

    xn = xn_ref[...]

    def up(s):
        extg[s][hp:hp + tm, :] = _dot(xn, wug_ref[:, cols(s)])
        extv[s][hp:hp + tm, :] = _dot(xn, wuv_ref[:, cols(s)])

    def gated(s):
        yg = _conv_from_ext(extg[s], cwg_ref[:, cols(s)], cbg_ref[:, cols(s)], FFN_CONV, dil, tm, hp)
        yv = _conv_from_ext(extv[s], cwv_ref[:, cols(s)], cbv_ref[:, cols(s)], FFN_CONV, dil, tm, hp)
        return (_gelu_doubled(yg) * yv).astype(BF16)

    ahead = min(2, n_sub)
    for s in range(ahead):
        up(s)
    acc = None
    for s in range(n_sub):
        if s + ahead < n_sub:
            up(s + ahead)
        down = _dot(gated(s), wd_ref[cols(s), :])
        acc = down if acc is None else acc + down
        for ext, car_ref, s_ref in ((extg[s], carg_ref, sg_ref), (extv[s], carv_ref, sv_ref)):
            tail = ext[tm:tm + hp, :]
            car_ref[c, :, cols(s)] = tail
            s_ref[0, :, cols(s)] = tail
    o_ref[...] += acc


def _ffn_call(x, geom, gain, w_up, conv_w, conv_b, prev, w_down, block, ck, mixer_out=None):
    tm, hp = geom.tm, geom.halo(FFN_CONV)
    nblk = D_FF // block
    assert sum(ck) == block
    tps = geom.tps
    resident = {"pipeline_mode": pl.Buffered(1)} if nblk == 1 else {}
    col = lambda r, off, **kw: pl.BlockSpec((r, block), lambda i, c: (0, off + c), **kw)
    st_in = lambda off: pl.BlockSpec((1, hp, block), lambda i, c: (i // tps, 0, off + c))
    st_out = pl.BlockSpec((1, hp, block), lambda i, c: (i, 0, c))
    rows_spec = lambda n: pl.BlockSpec((tm, n), lambda i, c: (i, 0))
    mixer, mixer_args, mixer_specs = None, [], []
    if mixer_out is not None:
        attn, yrg, w_attn, w_rg = mixer_out
        mixer = "transposed" if attn.ndim == 3 else "rows"
        a_spec = (pl.BlockSpec((1, A_WIDTH, tm), lambda i, c: (i // tps, 0, i % tps))
                  if mixer == "transposed" else rows_spec(A_WIDTH))
        w_spec = lambda w: pl.BlockSpec(w.shape, lambda i, c: (0, 0), pipeline_mode=pl.Buffered(1))
        mixer_args = [attn, yrg, w_attn, w_rg]
        mixer_specs = [a_spec, rows_spec(B_WIDTH), w_spec(w_attn), w_spec(w_rg)]
    kern = functools.partial(_ffn_kernel, tm=tm, tps=tps, dil=geom.dil, hp=hp, ck=ck, mixer=mixer)
    return pl.pallas_call(
        kern,
        grid=(geom.n_tiles, nblk),
        in_specs=[rows_spec(D_MODEL)] + mixer_specs + [
                  _full_spec(gain.shape),
                  col(D_MODEL, 0, **resident), col(D_MODEL, nblk, **resident),
                  col(FFN_CONV, 0), col(FFN_CONV, nblk),
                  col(1, 0), col(1, nblk),
                  st_in(0), st_in(nblk),
                  pl.BlockSpec((block, D_MODEL), lambda i, c: (c, 0), **resident)],
        out_specs=[pl.BlockSpec((tm, D_MODEL), lambda i, c: (i, 0)), st_out, st_out],
        out_shape=[jax.ShapeDtypeStruct((geom.rows, D_MODEL), F32),
                   jax.ShapeDtypeStruct((geom.n_tiles, hp, D_FF), F32),
                   jax.ShapeDtypeStruct((geom.n_tiles, hp, D_FF), F32)],
        scratch_shapes=[pltpu.VMEM((tm, D_MODEL), BF16),
                        pltpu.VMEM((nblk, hp, block), F32), pltpu.VMEM((nblk, hp, block), F32)]
        + [pltpu.VMEM((hp + tm, w), F32) for w in ck] * 2,
        compiler_params=_params(2),
    )(x, *mixer_args, gain, w_up, w_up, conv_w, conv_w, conv_b, conv_b, prev, prev, w_down)


def _ffn_pipelined_kernel(x_ref, *refs, tm, tps, n_tiles, dil, hp, ck, mixer):
    if mixer is not None:
        a_ref, y_ref, wa_ref, wy_ref = refs[:4]
        refs = refs[4:]
    (g_ref, wug_ref, wuv_ref, cwg_ref, cwv_ref, cbg_ref, cbv_ref, pgp_ref, pvp_ref, pgf_ref, pvf_ref,
     wd_ref, o_ref, sg_ref, sv_ref) = refs[:15]
    scratch = refs[15:]
    xres, xnb = scratch[0:2], scratch[2:4]
    ext0g, ext0v = scratch[4:6], scratch[6:8]
    n_sub = len(ck)
    extg = (None,) + tuple(scratch[8:8 + n_sub - 1])
    extv = (None,) + tuple(scratch[8 + n_sub - 1:8 + 2 * (n_sub - 1)])
    starts = [sum(ck[:s]) for s in range(n_sub)]
    cols = lambda s: slice(starts[s], starts[s] + ck[s])
    i = pl.program_id(0)
    tile_p = jnp.minimum(i, n_tiles - 1)
    tile_f = jnp.maximum(i - 1, 0)
    first_p = (tile_p % tps) == 0
    first_f = (tile_f % tps) == 0

    def prepare_norm(slot):
        x = x_ref[...]
        if mixer == "transposed":
            x = x + lax.dot_general(a_ref[0], wa_ref[...], (((0,), (0,)), ((), ())),
                                    preferred_element_type=F32)
        elif mixer == "rows":
            x = x + _dot(a_ref[...], wa_ref[...])
        if mixer is not None:
            x = x + _dot(y_ref[...], wy_ref[...])
        xres[slot][...] = x
        xnb[slot][...] = _rms(x, g_ref[...]).astype(BF16)

    def prepare_up(slot, halo_g, halo_v):
        xn = xnb[slot][...]
        ext0g[slot][0:hp, :] = halo_g
        ext0v[slot][0:hp, :] = halo_v
        ext0g[slot][hp:hp + tm, :] = _dot(xn, wug_ref[:, cols(0)])
        ext0v[slot][hp:hp + tm, :] = _dot(xn, wuv_ref[:, cols(0)])

    def finish_and_prepare(slot_f, slot_p):
        xn = xnb[slot_f][...]
        eg = (ext0g[slot_f],) + extg[1:]
        ev = (ext0v[slot_f],) + extv[1:]
        for s in range(1, n_sub):
            eg[s][0:hp, :] = jnp.where(first_f, pgf_ref[0, :, cols(s)], eg[s][tm:tm + hp, :])
            ev[s][0:hp, :] = jnp.where(first_f, pvf_ref[0, :, cols(s)], ev[s][tm:tm + hp, :])
        tail0g, tail0v = eg[0][tm:tm + hp, :], ev[0][tm:tm + hp, :]

        def up(s):
            eg[s][hp:hp + tm, :] = _dot(xn, wug_ref[:, cols(s)])
            ev[s][hp:hp + tm, :] = _dot(xn, wuv_ref[:, cols(s)])

        def gated(s):
            yg = _conv_from_ext(eg[s], cwg_ref[:, cols(s)], cbg_ref[:, cols(s)], FFN_CONV, dil, tm, hp)
            yv = _conv_from_ext(ev[s], cwv_ref[:, cols(s)], cbv_ref[:, cols(s)], FFN_CONV, dil, tm, hp)
            return (_gelu_doubled(yg) * yv).astype(BF16)

        prepare_norm(slot_p)
        acc = None
        for s in range(n_sub):
            if s + 1 < n_sub:
                up(s + 1)
            else:
                prepare_up(slot_p, jnp.where(first_p, pgp_ref[0, :, cols(0)], tail0g),
                           jnp.where(first_p, pvp_ref[0, :, cols(0)], tail0v))
            down = _dot(gated(s), wd_ref[cols(s), :])
            acc = down if acc is None else acc + down
        o_ref[...] = xres[slot_f][...] + acc
        sg_ref[0, :, cols(0)] = tail0g
        sv_ref[0, :, cols(0)] = tail0v
        for s in range(1, n_sub):
            sg_ref[0, :, cols(s)] = eg[s][tm:tm + hp, :]
            sv_ref[0, :, cols(s)] = ev[s][tm:tm + hp, :]

    @pl.when(i == 0)
    def _():
        prepare_norm(0)
        prepare_up(0, pgp_ref[0, :, cols(0)], pvp_ref[0, :, cols(0)])
        for s in range(1, n_sub):
            extg[s][tm:tm + hp, :] = jnp.zeros((hp, ck[s]), F32)
            extv[s][tm:tm + hp, :] = jnp.zeros((hp, ck[s]), F32)

    for parity in (0, 1):
        @pl.when((i > 0) & (i % 2 == parity))
        def _(parity=parity):
            finish_and_prepare(1 - parity, parity)


def _ffn_pipelined_call(x, geom, gain, w_up, conv_w, conv_b, prev, w_down, ck, mixer_out=None):
    tm, hp, tps, n_tiles = geom.tm, geom.halo(FFN_CONV), geom.tps, geom.n_tiles
    assert sum(ck) == D_FF and len(ck) >= 2 and geom.dil == 1
    tile_p = lambda i: jnp.minimum(i, n_tiles - 1)
    tile_f = lambda i: jnp.maximum(i - 1, 0)
    const = lambda a: pl.BlockSpec(a.shape, lambda i: (0,) * a.ndim, pipeline_mode=pl.Buffered(1))
    half = lambda r, off: pl.BlockSpec((r, D_FF), lambda i: (0, off), pipeline_mode=pl.Buffered(1))
    st_p = lambda off: pl.BlockSpec((1, hp, D_FF), lambda i: (tile_p(i) // tps, 0, off))
    st_f = lambda off: pl.BlockSpec((1, hp, D_FF), lambda i: (tile_f(i) // tps, 0, off))
    st_out = pl.BlockSpec((1, hp, D_FF), lambda i: (tile_f(i), 0, 0))
    rows_p = lambda n: pl.BlockSpec((tm, n), lambda i: (tile_p(i), 0))
    mixer, mixer_args, mixer_specs = None, [], []
    if mixer_out is not None:
        attn, yrg, w_attn, w_rg = mixer_out
        mixer = "transposed" if attn.ndim == 3 else "rows"
        a_spec = (pl.BlockSpec((1, A_WIDTH, tm), lambda i: (tile_p(i) // tps, 0, tile_p(i) % tps))
                  if mixer == "transposed" else rows_p(A_WIDTH))
        mixer_args = [attn, yrg, w_attn, w_rg]
        mixer_specs = [a_spec, rows_p(B_WIDTH), const(w_attn), const(w_rg)]
    kern = functools.partial(_ffn_pipelined_kernel, tm=tm, tps=tps, n_tiles=n_tiles, dil=geom.dil,
                             hp=hp, ck=ck, mixer=mixer)
    ext = lambda w: pltpu.VMEM((hp + tm, w), F32)
    return pl.pallas_call(
        kern,
        grid=(n_tiles + 1,),
        in_specs=[rows_p(D_MODEL)] + mixer_specs + [
                  const(gain),
                  half(D_MODEL, 0), half(D_MODEL, 1),
                  half(FFN_CONV, 0), half(FFN_CONV, 1),
                  half(1, 0), half(1, 1),
                  st_p(0), st_p(1), st_f(0), st_f(1),
                  const(w_down)],
        out_specs=[pl.BlockSpec((tm, D_MODEL), lambda i: (tile_f(i), 0)), st_out, st_out],
        out_shape=[jax.ShapeDtypeStruct((geom.rows, D_MODEL), F32),
                   jax.ShapeDtypeStruct((n_tiles, hp, D_FF), F32),
                   jax.ShapeDtypeStruct((n_tiles, hp, D_FF), F32)],
        scratch_shapes=[pltpu.VMEM((tm, D_MODEL), F32)] * 2 + [pltpu.VMEM((tm, D_MODEL), BF16)] * 2
        + [ext(ck[0])] * 4 + [ext(w) for w in ck[1:]] * 2,
        compiler_params=pltpu.CompilerParams(dimension_semantics=("arbitrary",),
                                             vmem_limit_bytes=FFN_PIPELINED_VMEM_LIMIT_BYTES),
    )(x, *mixer_args, gain, w_up, w_up, conv_w, conv_w, conv_b, conv_b, prev, prev, prev, prev, w_down)


def _state_to_halo(state, geom, hp):
    n, w1, ch = state.shape
    if geom.dil == 1:
        rows = state
    else:
        rows = state.transpose(1, 0, 2).reshape(1, w1 * n, ch)
    return jnp.pad(rows, ((0, 0), (hp - rows.shape[1], 0), (0, 0)))


def _halo_to_state(halo, geom, n, w1):
    ch = halo.shape[-1]
    if halo.shape[0] != geom.n_seq:
        halo = halo[geom.tps - 1::geom.tps]
    if geom.dil == 1:
        return halo[:, halo.shape[1] - w1:, :]
    return halo[0, halo.shape[1] - w1 * n:, :].reshape(w1, n, ch).transpose(1, 0, 2)


def _block_diag(w):
    nb, bi, bj = w.shape
    eye = jnp.eye(nb, dtype=w.dtype)
    return (eye[:, None, :, None] * w[:, :, None, :]).reshape(nb * bi, nb * bj)


def _trunk(x_rows, geom, n, pos0, p, paged, rg_h, rg_conv, pool_buf, ffn_buf, ffn_ck):
    depth = p["norm_mix"].shape[0]
    ks, vs, lfs, hs, cs, pbs, fbs = [], [], [], [], [], [], []
    dil = geom.dil
    t_steps = geom.rows // n
    dilp = _round_up(dil, SUBLANES)
    x = x_rows
    for layer in range(depth):
        li = layer // 2
        if layer % 2 == 0:
            w_in = p["ab_w_in"][li]
            a3 = 3 * A_WIDTH
            w = {
                "g": p["norm_mix"][layer][None, :],
                "wqk": w_in[:, :2 * A_WIDTH].astype(BF16),
                "wv": w_in[:, 2 * A_WIDTH:a3].astype(BF16),
                "wf": jnp.pad(w_in[:, a3:a3 + FOX_HEADS], ((0, 0), (0, F_PAD - FOX_HEADS))).astype(BF16),
                "wrg": w_in[:, a3 + FOX_HEADS:].astype(BF16),
                "bf": jnp.pad(p["ab_b_f"][li], (0, F_PAD - FOX_HEADS))[None, :],
                "qg": jnp.tile(p["ab_q_gain"][li], FOX_HEADS)[None, :],
                "kg": jnp.tile(p["ab_k_gain"][li], FOX_HEADS)[None, :],
                "bd": _block_diag(jnp.full((MXU_TILE // HEAD_DIM, HEAD_DIM, HEAD_DIM), 1.0 / HEAD_DIM,
                                           F32)).astype(BF16),
                "cw": p["ab_conv_w"][li], "cb": p["ab_conv_b"][li][None, :],
                "wa": _block_diag(p["ab_w_a"][li]).astype(BF16), "ba": p["ab_b_a"][li][None, :],
                "wx": _block_diag(p["ab_w_x"][li]).astype(BF16), "bx": p["ab_b_x"][li][None, :],
                "lam": p["ab_lambda"][li][None, :],
            }
            hp = geom.halo(RG_CONV)
            cprev = _state_to_halo(rg_conv[li], geom, hp)
            if dil == 1:
                h0 = jnp.pad(rg_h[li][:, None, :], ((0, 0), (0, dilp - 1), (0, 0)))
            else:
                h0 = jnp.pad(rg_h[li][None], ((0, 0), (0, dilp - dil), (0, 0)))
            scale = HEAD_DIM ** -0.5
            q_mult = scale * LOG2E if paged is None else scale
            q, k, v, kt, vt, lft, yrg, cst, hl = _ab_in_call(x, geom.with_tile(AB_IN_TILE), w, cprev, h0,
                                                             q_mult)

            if paged is None:
                shift = (SCORE_BOUND_SLACK * HEAD_DIM * q_mult * jnp.max(jnp.abs(p["ab_q_gain"][li]))
                         * jnp.max(jnp.abs(p["ab_k_gain"][li])) + 0.5)
                bounded = (shift < MAX_SAFE_SHIFT).astype(jnp.int32).reshape(1)
                cq, ck = _cumsum_call(lft.reshape(n * FOX_HEADS, t_steps), shift.reshape(1, 1))
                cq = cq.reshape(n, FOX_HEADS, GATE_ROWS, t_steps)
                ck = ck.reshape(n, FOX_HEADS, GATE_ROWS, t_steps)
                heads_t = lambda z: z.reshape(n, FOX_HEADS, HEAD_DIM, t_steps)
                attn = _attn_prompt_call(bounded, heads_t(q), cq, heads_t(k), ck, heads_t(v),
                                         min(512, t_steps))
                attn = attn.reshape(n, A_WIDTH, t_steps)
                ks.append(kt.reshape(n, FOX_HEADS, HEAD_DIM, t_steps).transpose(0, 3, 1, 2))
                vs.append(vt.reshape(n, FOX_HEADS, HEAD_DIM, t_steps).transpose(0, 3, 1, 2))
                lfs.append(lft.transpose(0, 2, 1))
                hs.append(hl[:, 0, :])
            else:
                ckt, cvt, clft, page_table = paged

                def heads_bm(z):
                    return z.reshape(t_steps, n, FOX_HEADS, HEAD_DIM).transpose(1, 2, 0, 3)
                lf_new = lft.transpose(2, 1, 0)[:, :, None, :]
                attn = _attn_sample_call(page_table, heads_bm(q), heads_bm(k), heads_bm(v), lf_new,
                                         ckt, cvt, clft, li)
                attn = attn.transpose(2, 0, 1, 3).reshape(geom.rows, A_WIDTH)
                ks.append(kt.reshape(t_steps, FOX_HEADS, HEAD_DIM, n).transpose(3, 0, 1, 2))
                vs.append(vt.reshape(t_steps, FOX_HEADS, HEAD_DIM, n).transpose(3, 0, 1, 2))
                lfs.append(lft.transpose(2, 0, 1))
                hs.append(hl[0, :dil, :])
            cs.append(_halo_to_state(cst, geom, n, RG_CONV - 1))
            w_out = p["ab_w_out"][li].astype(BF16)
            mixer_out = (attn, yrg, w_out[:A_WIDTH], w_out[A_WIDTH:])
        else:
            mixer_out = None
            hp = geom.halo(POOL_BUF + 1)
            prev = _state_to_halo(pool_buf[li], geom, hp)
            x, st = _pool_call(x, geom, p["norm_mix"][layer][None, :], prev,
                               p["pool_w"][li].astype(BF16), p["pool_scale"][li][None, :], pos0)
            pbs.append(_halo_to_state(st, geom, n, POOL_BUF))
        hp = geom.halo(FFN_CONV)
        prev = _state_to_halo(ffn_buf[layer], geom, hp)
        ffn_args = (x, geom, p["norm_ffn"][layer][None, :], p["ffn_w_up"][layer].astype(BF16),
                    p["ffn_conv_w"][layer], p["ffn_conv_b"][layer][None, :], prev,
                    (0.5 * p["ffn_w_down"][layer]).astype(BF16))
        block, ck = ffn_ck
        if block == D_FF and geom.n_tiles > 1 and dil == 1:
            x, sg, sv = _ffn_pipelined_call(*ffn_args, ck, mixer_out=mixer_out)
        else:
            x, sg, sv = _ffn_call(*ffn_args, block, ck, mixer_out=mixer_out)
        fbs.append(_halo_to_state(jnp.concatenate([sg, sv], axis=-1), geom, n, FFN_CONV - 1))
    return x, (jnp.stack(ks), jnp.stack(vs), jnp.stack(lfs), jnp.stack(hs), jnp.stack(cs),
               jnp.stack(pbs), jnp.stack(fbs))


def kernel(x_prompt, x_sample, cache_k, cache_v, cache_logf, state_rg_h, state_rg_conv, state_pool, state_ffn_conv, page_table, norm_mix, norm_ffn, ab_w_in, ab_b_f, ab_q_gain, ab_k_gain, ab_conv_w, ab_conv_b, ab_w_a, ab_b_a, ab_w_x, ab_b_x, ab_lambda, ab_w_out, pool_w, pool_scale, ffn_w_up, ffn_conv_w, ffn_conv_b, ffn_w_down):
    p = {
        "norm_mix": norm_mix, "norm_ffn": norm_ffn,
        "ab_w_in": ab_w_in, "ab_b_f": ab_b_f, "ab_q_gain": ab_q_gain, "ab_k_gain": ab_k_gain,
        "ab_conv_w": ab_conv_w, "ab_conv_b": ab_conv_b, "ab_w_a": ab_w_a, "ab_b_a": ab_b_a,
        "ab_w_x": ab_w_x, "ab_b_x": ab_b_x, "ab_lambda": ab_lambda, "ab_w_out": ab_w_out,
        "pool_w": pool_w, "pool_scale": pool_scale,
        "ffn_w_up": ffn_w_up, "ffn_conv_w": ffn_conv_w, "ffn_conv_b": ffn_conv_b, "ffn_w_down": ffn_w_down,
    }
    depth = norm_mix.shape[0]
    n_ab, n_pool = (depth + 1) // 2, depth // 2

    bsz, t, _ = x_prompt.shape
    geom_p = _Geom(bsz, t, 1, 512)
    y_p, st_p = _trunk(
        x_prompt.reshape(bsz * t, D_MODEL), geom_p, bsz, 0, p, None,
        jnp.zeros((n_ab, bsz, B_WIDTH), F32), jnp.zeros((n_ab, bsz, RG_CONV - 1, B_WIDTH), F32),
        jnp.zeros((n_pool, bsz, POOL_BUF, D_MODEL), F32),
        jnp.zeros((depth, bsz, FFN_CONV - 1, 2 * D_FF), F32), (D_FF, (1024, 1024, 1024)))
    y_prompt = y_p.reshape(bsz, t, D_MODEL)

    db, dt, _ = x_sample.shape
    n_pool_pages, page = cache_k.shape[1], cache_k.shape[2]
    past_len = page_table.shape[1] * page
    geom_s = _Geom(1, dt * db, db, 512)
    paged = (cache_k.transpose(0, 1, 3, 4, 2), cache_v.transpose(0, 1, 3, 4, 2),
             cache_logf.transpose(0, 1, 3, 2), page_table)
    y_s, st_s = _trunk(
        x_sample.transpose(1, 0, 2).reshape(dt * db, D_MODEL), geom_s, db, past_len, p, paged,
        state_rg_h, state_rg_conv, state_pool, state_ffn_conv, (512, (512,)))
    y_sample = y_s.reshape(dt, db, D_MODEL).transpose(1, 0, 2)
    return (y_prompt, y_sample) + st_p + st_s
```

```python
import functools

import jax
import jax.numpy as jnp
from jax import lax
from jax.experimental import pallas as pl
from jax.experimental.pallas import tpu as pltpu

D_MODEL = 1024
A_WIDTH = 512
B_WIDTH = 512
HEAD_DIM = 64
FOX_HEADS = 8
RG_CONV = 4
RG_C = 8.0
POOL_WINDOWS = (2, 4, 8, 16)
POOL_GD = 256
POOL_BUF = 15
D_FF = 3072
FFN_CONV = 3
EPS = 1e-6
NEG = -1e30
F_PAD = 128
GATE_ROWS = 16
ATTN_HEADS_PER_STEP = 2
SCORE_BOUND_SLACK = 1.02
MAX_SAFE_SHIFT = 56.0
LOG2E = 1.4426950408889634
SUBLANES = 8
MXU_TILE = 256
F32 = jnp.float32
BF16 = jnp.bfloat16
VMEM_LIMIT_BYTES = 56 * 1024 * 1024
FFN_PIPELINED_VMEM_LIMIT_BYTES = 62 * 1024 * 1024


def _round_up(x, m):
    return -(-x // m) * m


def _full_spec(shape):
    return pl.BlockSpec(shape, lambda *_: (0,) * len(shape))


def _params(n_axes):
    return pltpu.CompilerParams(dimension_semantics=("arbitrary",) * n_axes,
                                vmem_limit_bytes=VMEM_LIMIT_BYTES)


def _dot(a, b):
    return jnp.dot(a, b, preferred_element_type=F32)


def _dot_nt(a, b):
    return lax.dot_general(a, b, (((1,), (1,)), ((), ())), preferred_element_type=F32)


def _rms(x, gain):
    y = x * lax.rsqrt(jnp.mean(x * x, axis=-1, keepdims=True) + EPS)
    return y * gain


def _gelu(x):
    return 0.5 * x * (1.0 + jnp.tanh(0.7978845608028654 * (x + 0.044715 * (x * x * x))))


def _gelu_doubled(x):
    c = 0.7978845608028654
    return x * (1.0 + jnp.tanh(x * (c + (c * 0.044715) * (x * x))))


def _softplus(x):
    return jnp.maximum(x, 0.0) + jnp.log1p(jnp.exp(-jnp.abs(x)))


class _Geom:
    def __init__(self, n_seq, rows_per_seq, dil, tile):
        self.n_seq, self.rows_per_seq, self.dil = n_seq, rows_per_seq, dil
        self.tm = min(tile, rows_per_seq)
        assert rows_per_seq % self.tm == 0 and self.tm % dil == 0 and self.tm % SUBLANES == 0
        assert dil == 1 or dil % SUBLANES == 0
        self.tps = rows_per_seq // self.tm
        self.rows = n_seq * rows_per_seq
        self.n_tiles = self.rows // self.tm

    def halo(self, width):
        return _round_up((width - 1) * self.dil, SUBLANES)


def _conv_from_ext(ext_ref, w_ref, bias, width, dil, tm, hp):
    y = bias
    for j in range(width):
        off = hp - (width - 1 - j) * dil
        y = y + ext_ref[off:off + tm, :] * w_ref[j:j + 1, :]
    return y


def _ab_in_kernel(x_ref, g_ref, wqk_ref, wv_ref, wrg_ref, wf_ref, bf_ref, qg_ref, kg_ref, bd_ref,
                  cw_ref, cb_ref, wa_ref, ba_ref, wx_ref, bx_ref, lam_ref, cprev_ref, h0_ref,
                  q_out, k_out, v_out, kt_out, vt_out, lft_out, y_out, cst_out, h_out,
                  ext_ref, hc_ref, *, tm, tps, dil, hp, q_mult):
    i = pl.program_id(0)
    first = (i % tps) == 0
    xn = _rms(x_ref[...], g_ref[...]).astype(BF16)

    bd = bd_ref[...]

    def head_norm(z, gain):
        z2 = (z * z).astype(BF16)
        wide = bd.shape[0]
        ms = jnp.concatenate([_dot(z2[:, c:c + wide], bd) for c in range(0, A_WIDTH, wide)], axis=1)
        return z * lax.rsqrt(ms + EPS) * gain

    qk = _dot(xn, wqk_ref[...])
    q = head_norm(qk[:, :A_WIDTH], qg_ref[...])
    k = head_norm(qk[:, A_WIDTH:], kg_ref[...])
    v = _dot(xn, wv_ref[...])
    q = q * q_mult

    f = _dot(xn, wf_ref[...]) + bf_ref[...]
    lf = jnp.minimum(f, 0.0) - jnp.log1p(jnp.exp(-jnp.abs(f)))

    unit = tm if dil == 1 else dil
    for u in range(tm // unit):
        rows_u = slice(u * unit, (u + 1) * unit)
        kt = k[rows_u, :].T
        vt = v[rows_u, :].T
        kt_out[u] = kt
        vt_out[u] = vt
        lft_out[u] = lf[rows_u, :].T[:FOX_HEADS, :]
        if dil == 1:
            q_out[u] = q.T.astype(BF16)
            k_out[u] = kt.astype(BF16)
            v_out[u] = vt.astype(BF16)
    if dil != 1:
        q_out[...] = q.astype(BF16)
        k_out[...] = k.astype(BF16)
        v_out[...] = v.astype(BF16)

    rg = _dot(xn, wrg_ref[...])
    xr = rg[:, :B_WIDTH]
    gate = rg[:, B_WIDTH:]

    @pl.when(first)
    def _():
        ext_ref[0:hp, :] = cprev_ref[0]
        hc_ref[...] = h0_ref[0]

    ext_ref[hp:hp + tm, :] = xr
    xc = _conv_from_ext(ext_ref, cw_ref, cb_ref[...], RG_CONV, dil, tm, hp)
    tail = ext_ref[tm:tm + hp, :]
    ext_ref[0:hp, :] = tail
    cst_out[0] = tail

    xcb = xc.astype(BF16)
    r = jax.nn.sigmoid(_dot(xcb, wa_ref[...]) + ba_ref[...])
    gi = jax.nn.sigmoid(_dot(xcb, wx_ref[...]) + bx_ref[...])
    log_a = -RG_C * r * _softplus(-lam_ref[...])
    a = jnp.exp(log_a)
    inp = jnp.sqrt(-jnp.tanh(log_a) * (a * a + 1.0)) * gi * xc

    unit = max(dil, SUBLANES)
    sub = lax.broadcasted_iota(jnp.int32, (tm, 1), 0) % unit
    s = dil
    while s < unit:
        valid = sub >= s
        a_sh = jnp.where(valid, pltpu.roll(a, s, 0), 1.0)
        h_sh = jnp.where(valid, pltpu.roll(inp, s, 0), 0.0)
        inp = a * h_sh + inp
        a = a * a_sh
        s *= 2

    carry = hc_ref[0:dil, :]
    groups = []
    for g in range(tm // unit):
        rows_g = slice(g * unit, (g + 1) * unit)
        incoming = jnp.broadcast_to(carry, (unit, B_WIDTH)) if dil < unit else carry
        hg = inp[rows_g, :] + a[rows_g, :] * incoming
        groups.append(hg)
        carry = hg[unit - dil:unit, :]
    h = jnp.concatenate(groups, axis=0)
    hc_ref[0:dil, :] = carry
    h_out[0] = hc_ref[...]
    y_out[...] = (h * _gelu(gate)).astype(BF16)


def _ab_in_call(x, geom, w, cprev, h0, q_mult):
    tm, hp, tps = geom.tm, geom.halo(RG_CONV), geom.tps
    dilp = _round_up(geom.dil, SUBLANES)
    rows = geom.rows
    tile = lambda n: pl.BlockSpec((tm, n), lambda i: (i, 0))
    seq3 = lambda r, n: pl.BlockSpec((1, r, n), lambda i: (i // tps, 0, 0))
    if geom.dil == 1:
        t_shape = lambda ch: (geom.n_seq, ch, geom.rows_per_seq)
        t_spec = lambda ch: pl.BlockSpec((1, ch, tm), lambda i: (i // tps, 0, i % tps))
        op_spec, op_shape = t_spec(A_WIDTH), jax.ShapeDtypeStruct(t_shape(A_WIDTH), BF16)
    else:
        t_shape = lambda ch: (rows // geom.dil, ch, geom.dil)
        t_spec = lambda ch: pl.BlockSpec((tm // geom.dil, ch, geom.dil), lambda i: (i, 0, 0))
        op_spec, op_shape = tile(A_WIDTH), jax.ShapeDtypeStruct((rows, A_WIDTH), BF16)
    consts = [w["g"], w["wqk"], w["wv"], w["wrg"], w["wf"], w["bf"], w["qg"], w["kg"], w["bd"],
              w["cw"], w["cb"], w["wa"], w["ba"], w["wx"], w["bx"], w["lam"]]
    kern = functools.partial(_ab_in_kernel, tm=tm, tps=tps, dil=geom.dil, hp=hp, q_mult=q_mult)
    return pl.pallas_call(
        kern,
        grid=(geom.n_tiles,),
        in_specs=[tile(D_MODEL)] + [_full_spec(c.shape) for c in consts]
        + [seq3(hp, B_WIDTH), seq3(dilp, B_WIDTH)],
        out_specs=[op_spec, op_spec, op_spec,
                   t_spec(A_WIDTH), t_spec(A_WIDTH), t_spec(FOX_HEADS), tile(B_WIDTH),
                   seq3(hp, B_WIDTH), seq3(dilp, B_WIDTH)],
        out_shape=[op_shape, op_shape, op_shape,
                   jax.ShapeDtypeStruct(t_shape(A_WIDTH), F32),
                   jax.ShapeDtypeStruct(t_shape(A_WIDTH), F32),
                   jax.ShapeDtypeStruct(t_shape(FOX_HEADS), F32),
                   jax.ShapeDtypeStruct((rows, B_WIDTH), BF16),
                   jax.ShapeDtypeStruct((geom.n_seq, hp, B_WIDTH), F32),
                   jax.ShapeDtypeStruct((geom.n_seq, dilp, B_WIDTH), F32)],
        scratch_shapes=[pltpu.VMEM((hp + tm, B_WIDTH), F32), pltpu.VMEM((dilp, B_WIDTH), F32)],
        compiler_params=_params(1),
    )(x, *consts, cprev, h0)


def _cumsum_kernel(x_ref, shift_ref, cq_ref, ck_ref, *, n):
    x = x_ref[...]
    lane = lax.broadcasted_iota(jnp.int32, x.shape, 1)
    s = 1
    while s < n:
        x = x + jnp.where(lane >= s, pltpu.roll(x, s, 1), 0.0)
        s *= 2
    x = x * LOG2E

    def split(v):
        hi = v.astype(BF16).astype(F32)
        rest = v - hi
        mid = rest.astype(BF16).astype(F32)
        return hi, mid, rest - mid

    q_parts, k_parts = split(x - shift_ref[...]), split(x)
    row = lax.broadcasted_iota(jnp.int32, (GATE_ROWS, n), 0)
    for r in range(x.shape[0]):
        cq = jnp.where(row < 3, -1.0, 0.0)
        ck = jnp.where((row >= 3) & (row < 6), 1.0, 0.0)
        for j in range(3):
            cq = jnp.where(row == 3 + j, jnp.broadcast_to(q_parts[j][r:r + 1, :], (GATE_ROWS, n)), cq)
            ck = jnp.where(row == j, jnp.broadcast_to(k_parts[j][r:r + 1, :], (GATE_ROWS, n)), ck)
        cq_ref[r] = cq.astype(BF16)
        ck_ref[r] = ck.astype(BF16)


def _cumsum_call(x, shift):
    rows, n = x.shape
    out = jax.ShapeDtypeStruct((rows, GATE_ROWS, n), BF16)
    return pl.pallas_call(
        functools.partial(_cumsum_kernel, n=n),
        grid=(1,),
        in_specs=[_full_spec(x.shape), _full_spec(shift.shape)],
        out_specs=[_full_spec(out.shape)] * 2,
        out_shape=[out] * 2,
        compiler_params=_params(1),
    )(x, shift)


def _attn_prompt_kernel(bounded_ref, qt_ref, cq_ref, kt_ref, ck_ref, vt_ref, o_ref, *, tq, sb, depth):
    for online in (False, True):
        @pl.when((bounded_ref[0] == 0) == online)
        def _(online=online):
            _attn_prompt_body(qt_ref, cq_ref, kt_ref, ck_ref, vt_ref, o_ref,
                              tq=tq, sb=sb, depth=depth, online=online)


def _attn_prompt_body(qt_ref, cq_ref, kt_ref, ck_ref, vt_ref, o_ref, *, tq, sb, depth, online):
    qi = pl.program_id(2)
    n_heads = qt_ref.shape[1]
    heads = range(n_heads)
    qt = [jnp.concatenate([qt_ref[0, h], cq_ref[0, h]], axis=0) for h in heads]
    n_sb = tq // sb
    row = lax.broadcasted_iota(jnp.int32, (sb, tq), 0)
    col = lax.broadcasted_iota(jnp.int32, (sb, tq), 1)

    def key_rows(j, s):
        return pl.ds(pl.multiple_of(j * tq + s * sb, sb), sb)

    def scores(h, j, s):
        kt = jnp.concatenate([kt_ref[0, h, :, key_rows(j, s)], ck_ref[0, h, :, key_rows(j, s)]], axis=0)
        return lax.dot_general(kt, qt[h], (((0,), (0,)), ((), ())), preferred_element_type=F32)

    def chunk(j, carry, diagonal):
        state, ahead = list(carry[0]), [list(a) for a in carry[1]]
        for s in range(n_sb):
            for h in heads:
                m, l, acc = state[h]
                st = ahead[h].pop(0)
                if s + depth < n_sb:
                    ahead[h].append(scores(h, j, s + depth))
                elif not diagonal:
                    ahead[h].append(scores(h, j + 1, s + depth - n_sb))
                if diagonal:
                    st = jnp.where(row + s * sb <= col, st, NEG)
                if online:
                    m_new = jnp.maximum(m, jnp.max(st, axis=0, keepdims=True))
                    alpha = jnp.exp2(m - m_new)
                    p = jnp.exp2(st - m_new)
                    l = alpha * l + jnp.sum(p, axis=0, keepdims=True)
                    acc = alpha * acc + _dot(vt_ref[0, h, :, key_rows(j, s)], p.astype(BF16))
                    m = m_new
                else:
                    p = jnp.exp2(st)
                    l = l + jnp.sum(p, axis=0, keepdims=True)
                    acc = acc + _dot(vt_ref[0, h, :, key_rows(j, s)], p.astype(BF16))
                state[h] = (m, l, acc)
        return tuple(state), tuple(tuple(a) for a in ahead)

    state = tuple((jnp.full((1, tq), NEG, F32), jnp.zeros((1, tq), F32), jnp.zeros((HEAD_DIM, tq), F32))
                  for _ in heads)
    carry = (state, tuple(tuple(scores(h, 0, s) for s in range(depth)) for h in heads))
    carry = lax.fori_loop(0, qi, lambda j, c: chunk(j, c, False), carry)
    state, _ = chunk(qi, carry, True)
    for h, (_, l, acc) in enumerate(state):
        o_ref[0, h] = (acc / l).astype(BF16)


def _attn_prompt_call(bounded, q_t, cq, k_t, ck, v_t, tq):
    bsz, nh, dh, t = q_t.shape
    sb = min(128, tq)
    depth = min(2, tq // sb)
    hg = ATTN_HEADS_PER_STEP
    tile = lambda r: pl.BlockSpec((1, hg, r, tq), lambda b, h, i, flag: (b, h, 0, i))
    whole = lambda r: pl.BlockSpec((1, hg, r, t), lambda b, h, i, flag: (b, h, 0, 0))
    grid_spec = pltpu.PrefetchScalarGridSpec(
        num_scalar_prefetch=1, grid=(bsz, nh // hg, t // tq),
        in_specs=[tile(dh), tile(GATE_ROWS), whole(dh), whole(GATE_ROWS), whole(dh)],
        out_specs=tile(dh))
    return pl.pallas_call(
        functools.partial(_attn_prompt_kernel, tq=tq, sb=sb, depth=depth),
        grid_spec=grid_spec,
        out_shape=jax.ShapeDtypeStruct((bsz, nh, dh, t), BF16),
        compiler_params=_params(3),
    )(bounded, q_t, cq, k_t, ck, v_t)


def _attn_sample_kernel(pt_ref, q_ref, kn_ref, vn_ref, lfn_ref, *refs, n_pages, page, dt):
    del pt_ref
    k_refs = refs[:n_pages]
    v_refs = refs[n_pages:2 * n_pages]
    lf_refs = refs[2 * n_pages:3 * n_pages]
    o_ref = refs[3 * n_pages]
    nh = FOX_HEADS
    past = n_pages * page
    bdot = lambda a, b: lax.dot_general(a, b, (((2,), (1,)), ((0,), (0,))), preferred_element_type=F32)
    bdot_nt = lambda a, b: lax.dot_general(a, b, (((2,), (2,)), ((0,), (0,))), preferred_element_type=F32)

    q3 = q_ref[0]

    lft = jnp.concatenate([r[0, 0] for r in lf_refs], axis=1)
    lane = lax.broadcasted_iota(jnp.int32, lft.shape, 1)
    suf = lft
    s = 1
    while s < past:
        suf = suf + jnp.where(lane < past - s, pltpu.roll(suf, past - s, 1), 0.0)
        s *= 2
    suf = suf - lft
    suf3 = jnp.stack([jnp.broadcast_to(suf[h:h + 1, :], (dt, past)) for h in range(nh)], axis=0)

    lnew = jnp.broadcast_to(lfn_ref[0], (nh, dt, dt))
    colq = lax.broadcasted_iota(jnp.int32, (nh, dt, dt), 2)
    tq = lax.broadcasted_iota(jnp.int32, (nh, dt, dt), 1)
    causal = colq <= tq
    nq = jnp.sum(jnp.where(causal, lnew, 0.0), axis=2, keepdims=True)
    g = jnp.zeros((nh, dt, dt), F32)
    for l in range(dt):
        g = g + jnp.where(colq >= l, lnew[:, :, l:l + 1], 0.0)

    kt_all = jnp.concatenate([r[0, 0].astype(BF16) for r in k_refs], axis=2)
    vt_all = jnp.concatenate([r[0, 0].astype(BF16) for r in v_refs], axis=2)
    s_past = bdot(q3, kt_all) + (suf3 + nq)
    s_new = jnp.where(causal, bdot_nt(q3, kn_ref[0]) + (nq - g), NEG)

    m = jnp.maximum(jnp.max(s_new, axis=2, keepdims=True), jnp.max(s_past, axis=2, keepdims=True))
    p_new = jnp.exp(s_new - m)
    p_past = jnp.exp(s_past - m)
    l = jnp.sum(p_new, axis=2, keepdims=True) + jnp.sum(p_past, axis=2, keepdims=True)
    acc = bdot(p_new.astype(BF16), vn_ref[0]) + bdot_nt(p_past.astype(BF16), vt_all)
    o_ref[0] = (acc / l).astype(BF16)


def _attn_sample_call(page_table, q, k_new, v_new, lf_new, cache_kt, cache_vt, cache_lft, layer):
    db, nh, dt, dh = q.shape
    n_pages = page_table.shape[1]
    page = cache_kt.shape[-1]
    new_spec = pl.BlockSpec((1, nh, dt, dh), lambda b, pt: (b, 0, 0, 0))

    def page_spec(shape, p):
        zeros = (0,) * len(shape)
        return pl.BlockSpec((1, 1) + shape, lambda b, pt: (layer, pt[b, p]) + zeros)

    in_specs = ([new_spec, new_spec, new_spec,
                 pl.BlockSpec((1, nh, 1, dt), lambda b, pt: (b, 0, 0, 0))]
                + [page_spec((nh, dh, page), p) for p in range(n_pages)]
                + [page_spec((nh, dh, page), p) for p in range(n_pages)]
                + [page_spec((nh, page), p) for p in range(n_pages)])
    grid_spec = pltpu.PrefetchScalarGridSpec(
        num_scalar_prefetch=1, grid=(db,), in_specs=in_specs, out_specs=new_spec)
    return pl.pallas_call(
        functools.partial(_attn_sample_kernel, n_pages=n_pages, page=page, dt=dt),
        grid_spec=grid_spec,
        out_shape=jax.ShapeDtypeStruct((db, nh, dt, dh), BF16),
        compiler_params=_params(1),
    )(page_table, q, k_new, v_new, lf_new,
      *([cache_kt] * n_pages), *([cache_vt] * n_pages), *([cache_lft] * n_pages))


def _pool_kernel(x_ref, gain_ref, prev_ref, w_ref, scale_ref, o_ref, st_ref, ext_ref,
                 *, tm, tps, dil, hp, pos0):
    i = pl.program_id(0)
    first = (i % tps) == 0
    x = x_ref[...]
    xn = _rms(x, gain_ref[...])
    rows = lax.broadcasted_iota(jnp.int32, (tm, 1), 0) + (i % tps) * tm
    pos = lax.div(rows, jnp.int32(dil)) + pos0

    @pl.when(first)
    def _():
        ext_ref[0:hp, :] = prev_ref[0]

    ext_ref[hp:hp + tm, :] = xn
    for gi, window in enumerate(POOL_WINDOWS):
        cols = slice(gi * POOL_GD, (gi + 1) * POOL_GD)
        xg = xn[:, cols]
        total = xg
        for j in range(1, window):
            total = total + ext_ref[hp - j * dil:hp - j * dil + tm, cols]
        cnt = jnp.minimum(pos + 1, window).astype(F32)
        diff = total / cnt - xg
        y = _dot(diff.astype(BF16), w_ref[gi]) * scale_ref[:, cols]
        o_ref[:, cols] = x[:, cols] + y
    tail = ext_ref[tm:tm + hp, :]
    ext_ref[0:hp, :] = tail
    st_ref[0] = tail


def _pool_call(x, geom, gain, prev, w, scale, pos0):
    tm, hp = geom.tm, geom.halo(POOL_BUF + 1)
    kern = functools.partial(_pool_kernel, tm=tm, tps=geom.tps, dil=geom.dil, hp=hp, pos0=pos0)
    resident = {"pipeline_mode": pl.Buffered(1)} if geom.n_seq == 1 else {}
    return pl.pallas_call(
        kern,
        grid=(geom.n_tiles,),
        in_specs=[pl.BlockSpec((tm, D_MODEL), lambda i: (i, 0)),
                  _full_spec(gain.shape),
                  pl.BlockSpec((1, hp, D_MODEL), lambda i: (i // geom.tps, 0, 0), **resident),
                  _full_spec(w.shape),
                  _full_spec(scale.shape)],
        out_specs=[pl.BlockSpec((tm, D_MODEL), lambda i: (i, 0)),
                   pl.BlockSpec((1, hp, D_MODEL), lambda i: (i, 0, 0))],
        out_shape=[jax.ShapeDtypeStruct((geom.rows, D_MODEL), F32),
                   jax.ShapeDtypeStruct((geom.n_tiles, hp, D_MODEL), F32)],
        scratch_shapes=[pltpu.VMEM((hp + tm, D_MODEL), F32)],
        compiler_params=_params(1),
    )(x, gain, prev, w, scale)


def _ffn_kernel(x_ref, *refs, tm, tps, dil, hp, ck, mixer):
    if mixer is not None:
        a_ref, y_ref, wa_ref, wy_ref = refs[:4]
        refs = refs[4:]
    (g_ref, wug_ref, wuv_ref, cwg_ref, cwv_ref, cbg_ref, cbv_ref, pg_ref, pv_ref, wd_ref,
     o_ref, sg_ref, sv_ref, xn_ref, carg_ref, carv_ref) = refs[:16]
    ext_refs = refs[16:]
    i = pl.program_id(0)
    c = pl.program_id(1)
    first = (i % tps) == 0
    n_sub = len(ext_refs) // 2
    extg, extv = ext_refs[:n_sub], ext_refs[n_sub:]
    starts = [sum(ck[:s]) for s in range(n_sub)]
    cols = lambda s: slice(starts[s], starts[s] + ck[s])

    @pl.when(c == 0)
    def _():
        x = x_ref[...]
        if mixer == "transposed":
            x = x + lax.dot_general(a_ref[0], wa_ref[...], (((0,), (0,)), ((), ())),
                                    preferred_element_type=F32)
        elif mixer == "rows":
            x = x + _dot(a_ref[...], wa_ref[...])
        if mixer is not None:
            x = x + _dot(y_ref[...], wy_ref[...])
        xn_ref[...] = _rms(x, g_ref[...]).astype(BF16)
        o_ref[...] = x

    @pl.when(first)
    def _():
        for s in range(n_sub):
            extg[s][0:hp, :] = pg_ref[0, :, cols(s)]
            extv[s][0:hp, :] = pv_ref[0, :, cols(s)]

    @pl.when(jnp.logical_not(first))
    def _():
        for s in range(n_sub):
            extg[s][0:hp, :] = carg_ref[c, :, cols(s)]
            extv[s][0:hp, :] = carv_ref[c, :, cols(s)]

    xn = xn_ref[...]

    def up(s):
        extg[s][hp:hp + tm, :] = _dot(xn, wug_ref[:, cols(s)])
        extv[s][hp:hp + tm, :] = _dot(xn, wuv_ref[:, cols(s)])

    def gated(s):
        yg = _conv_from_ext(extg[s], cwg_ref[:, cols(s)], cbg_ref[:, cols(s)], FFN_CONV, dil, tm, hp)
        yv = _conv_from_ext(extv[s], cwv_ref[:, cols(s)], cbv_ref[:, cols(s)], FFN_CONV, dil, tm, hp)
        return (_gelu_doubled(yg) * yv).astype(BF16)

    ahead = min(2, n_sub)
    for s in range(ahead):
        up(s)
    acc = None
    for s in range(n_sub):
        if s + ahead < n_sub:
            up(s + ahead)
        down = _dot(gated(s), wd_ref[cols(s), :])
        acc = down if acc is None else acc + down
        for ext, car_ref, s_ref in ((extg[s], carg_ref, sg_ref), (extv[s], carv_ref, sv_ref)):
            tail = ext[tm:tm + hp, :]
            car_ref[c, :, cols(s)] = tail
            s_ref[0, :, cols(s)] = tail
    o_ref[...] += acc


def _ffn_call(x, geom, gain, w_up, conv_w, conv_b, prev, w_down, block, ck, mixer_out=None):
    tm, hp = geom.tm, geom.halo(FFN_CONV)
    nblk = D_FF // block
    assert sum(ck) == block
    tps = geom.tps
    resident = {"pipeline_mode": pl.Buffered(1)} if nblk == 1 else {}
    col = lambda r, off, **kw: pl.BlockSpec((r, block), lambda i, c: (0, off + c), **kw)
    st_in = lambda off: pl.BlockSpec((1, hp, block), lambda i, c: (i // tps, 0, off + c))
    st_out = pl.BlockSpec((1, hp, block), lambda i, c: (i, 0, c))
    rows_spec = lambda n: pl.BlockSpec((tm, n), lambda i, c: (i, 0))
    mixer, mixer_args, mixer_specs = None, [], []
    if mixer_out is not None:
        attn, yrg, w_attn, w_rg = mixer_out
        mixer = "transposed" if attn.ndim == 3 else "rows"
        a_spec = (pl.BlockSpec((1, A_WIDTH, tm), lambda i, c: (i // tps, 0, i % tps))
                  if mixer == "transposed" else rows_spec(A_WIDTH))
        w_spec = lambda w: pl.BlockSpec(w.shape, lambda i, c: (0, 0), pipeline_mode=pl.Buffered(1))
        mixer_args = [attn, yrg, w_attn, w_rg]
        mixer_specs = [a_spec, rows_spec(B_WIDTH), w_spec(w_attn), w_spec(w_rg)]
    kern = functools.partial(_ffn_kernel, tm=tm, tps=tps, dil=geom.dil, hp=hp, ck=ck, mixer=mixer)
    return pl.pallas_call(
        kern,
        grid=(geom.n_tiles, nblk),
        in_specs=[rows_spec(D_MODEL)] + mixer_specs + [
                  _full_spec(gain.shape),
                  col(D_MODEL, 0, **resident), col(D_MODEL, nblk, **resident),
                  col(FFN_CONV, 0), col(FFN_CONV, nblk),
                  col(1, 0), col(1, nblk),
                  st_in(0), st_in(nblk),
                  pl.BlockSpec((block, D_MODEL), lambda i, c: (c, 0), **resident)],
        out_specs=[pl.BlockSpec((tm, D_MODEL), lambda i, c: (i, 0)), st_out, st_out],
        out_shape=[jax.ShapeDtypeStruct((geom.rows, D_MODEL), F32),
                   jax.ShapeDtypeStruct((geom.n_tiles, hp, D_FF), F32),
                   jax.ShapeDtypeStruct((geom.n_tiles, hp, D_FF), F32)],
        scratch_shapes=[pltpu.VMEM((tm, D_MODEL), BF16),
                        pltpu.VMEM((nblk, hp, block), F32), pltpu.VMEM((nblk, hp, block), F32)]
        + [pltpu.VMEM((hp + tm, w), F32) for w in ck] * 2,
        compiler_params=_params(2),
    )(x, *mixer_args, gain, w_up, w_up, conv_w, conv_w, conv_b, conv_b, prev, prev, w_down)


def _ffn_pipelined_kernel(x_ref, *refs, tm, tps, n_tiles, dil, hp, ck, mixer):
    if mixer is not None:
        a_ref, y_ref, wa_ref, wy_ref = refs[:4]
        refs = refs[4:]
    (g_ref, wug_ref, wuv_ref, cwg_ref, cwv_ref, cbg_ref, cbv_ref, pgp_ref, pvp_ref, pgf_ref, pvf_ref,
     wd_ref, o_ref, sg_ref, sv_ref) = refs[:15]
    scratch = refs[15:]
    xres, xnb = scratch[0:2], scratch[2:4]
    ext0g, ext0v = scratch[4:6], scratch[6:8]
    n_sub = len(ck)
    extg = (None,) + tuple(scratch[8:8 + n_sub - 1])
    extv = (None,) + tuple(scratch[8 + n_sub - 1:8 + 2 * (n_sub - 1)])
    starts = [sum(ck[:s]) for s in range(n_sub)]
    cols = lambda s: slice(starts[s], starts[s] + ck[s])
    i = pl.program_id(0)
    tile_p = jnp.minimum(i, n_tiles - 1)
    tile_f = jnp.maximum(i - 1, 0)
    first_p = (tile_p % tps) == 0
    first_f = (tile_f % tps) == 0

    def prepare_norm(slot):
        x = x_ref[...]
        if mixer == "transposed":
            x = x + lax.dot_general(a_ref[0], wa_ref[...], (((0,), (0,)), ((), ())),
                                    preferred_element_type=F32)
        elif mixer == "rows":
            x = x + _dot(a_ref[...], wa_ref[...])
        if mixer is not None:
            x = x + _dot(y_ref[...], wy_ref[...])
        xres[slot][...] = x
        xnb[slot][...] = _rms(x, g_ref[...]).astype(BF16)

    def prepare_up(slot, halo_g, halo_v):
        xn = xnb[slot][...]
        ext0g[slot][0:hp, :] = halo_g
        ext0v[slot][0:hp, :] = halo_v
        ext0g[slot][hp:hp + tm, :] = _dot(xn, wug_ref[:, cols(0)])
        ext0v[slot][hp:hp + tm, :] = _dot(xn, wuv_ref[:, cols(0)])

    def finish_and_prepare(slot_f, slot_p):
        xn = xnb[slot_f][...]
        eg = (ext0g[slot_f],) + extg[1:]
        ev = (ext0v[slot_f],) + extv[1:]
        for s in range(1, n_sub):
            eg[s][0:hp, :] = jnp.where(first_f, pgf_ref[0, :, cols(s)], eg[s][tm:tm + hp, :])
            ev[s][0:hp, :] = jnp.where(first_f, pvf_ref[0, :, cols(s)], ev[s][tm:tm + hp, :])
        tail0g, tail0v = eg[0][tm:tm + hp, :], ev[0][tm:tm + hp, :]

        def up(s):
            eg[s][hp:hp + tm, :] = _dot(xn, wug_ref[:, cols(s)])
            ev[s][hp:hp + tm, :] = _dot(xn, wuv_ref[:, cols(s)])

        def gated(s):
            yg = _conv_from_ext(eg[s], cwg_ref[:, cols(s)], cbg_ref[:, cols(s)], FFN_CONV, dil, tm, hp)
            yv = _conv_from_ext(ev[s], cwv_ref[:, cols(s)], cbv_ref[:, cols(s)], FFN_CONV, dil, tm, hp)
            return (_gelu_doubled(yg) * yv).astype(BF16)

        prepare_norm(slot_p)
        acc = None
        for s in range(n_sub):
            if s + 1 < n_sub:
                up(s + 1)
            else:
                prepare_up(slot_p, jnp.where(first_p, pgp_ref[0, :, cols(0)], tail0g),
                           jnp.where(first_p, pvp_ref[0, :, cols(0)], tail0v))
            down = _dot(gated(s), wd_ref[cols(s), :])
            acc = down if acc is None else acc + down
        o_ref[...] = xres[slot_f][...] + acc
        sg_ref[0, :, cols(0)] = tail0g
        sv_ref[0, :, cols(0)] = tail0v
        for s in range(1, n_sub):
            sg_ref[0, :, cols(s)] = eg[s][tm:tm + hp, :]
            sv_ref[0, :, cols(s)] = ev[s][tm:tm + hp, :]

    @pl.when(i == 0)
    def _():
        prepare_norm(0)
        prepare_up(0, pgp_ref[0, :, cols(0)], pvp_ref[0, :, cols(0)])
        for s in range(1, n_sub):
            extg[s][tm:tm + hp, :] = jnp.zeros((hp, ck[s]), F32)
            extv[s][tm:tm + hp, :] = jnp.zeros((hp, ck[s]), F32)

    for parity in (0, 1):
        @pl.when((i > 0) & (i % 2 == parity))
        def _(parity=parity):
            finish_and_prepare(1 - parity, parity)


def _ffn_pipelined_call(x, geom, gain, w_up, conv_w, conv_b, prev, w_down, ck, mixer_out=None):
    tm, hp, tps, n_tiles = geom.tm, geom.halo(FFN_CONV), geom.tps, geom.n_tiles
    assert sum(ck) == D_FF and len(ck) >= 2 and geom.dil == 1
    tile_p = lambda i: jnp.minimum(i, n_tiles - 1)
    tile_f = lambda i: jnp.maximum(i - 1, 0)
    const = lambda a: pl.BlockSpec(a.shape, lambda i: (0,) * a.ndim, pipeline_mode=pl.Buffered(1))
    half = lambda r, off: pl.BlockSpec((r, D_FF), lambda i: (0, off), pipeline_mode=pl.Buffered(1))
    st_p = lambda off: pl.BlockSpec((1, hp, D_FF), lambda i: (tile_p(i) // tps, 0, off))
    st_f = lambda off: pl.BlockSpec((1, hp, D_FF), lambda i: (tile_f(i) // tps, 0, off))
    st_out = pl.BlockSpec((1, hp, D_FF), lambda i: (tile_f(i), 0, 0))
    rows_p = lambda n: pl.BlockSpec((tm, n), lambda i: (tile_p(i), 0))
    mixer, mixer_args, mixer_specs = None, [], []
    if mixer_out is not None:
        attn, yrg, w_attn, w_rg = mixer_out
        mixer = "transposed" if attn.ndim == 3 else "rows"
        a_spec = (pl.BlockSpec((1, A_WIDTH, tm), lambda i: (tile_p(i) // tps, 0, tile_p(i) % tps))
                  if mixer == "transposed" else rows_p(A_WIDTH))
        mixer_args = [attn, yrg, w_attn, w_rg]
        mixer_specs = [a_spec, rows_p(B_WIDTH), const(w_attn), const(w_rg)]
    kern = functools.partial(_ffn_pipelined_kernel, tm=tm, tps=tps, n_tiles=n_tiles, dil=geom.dil,
                             hp=hp, ck=ck, mixer=mixer)
    ext = lambda w: pltpu.VMEM((hp + tm, w), F32)
    return pl.pallas_call(
        kern,
        grid=(n_tiles + 1,),
        in_specs=[rows_p(D_MODEL)] + mixer_specs + [
                  const(gain),
                  half(D_MODEL, 0), half(D_MODEL, 1),
                  half(FFN_CONV, 0), half(FFN_CONV, 1),
                  half(1, 0), half(1, 1),
                  st_p(0), st_p(1), st_f(0), st_f(1),
                  const(w_down)],
        out_specs=[pl.BlockSpec((tm, D_MODEL), lambda i: (tile_f(i), 0)), st_out, st_out],
        out_shape=[jax.ShapeDtypeStruct((geom.rows, D_MODEL), F32),
                   jax.ShapeDtypeStruct((n_tiles, hp, D_FF), F32),
                   jax.ShapeDtypeStruct((n_tiles, hp, D_FF), F32)],
        scratch_shapes=[pltpu.VMEM((tm, D_MODEL), F32)] * 2 + [pltpu.VMEM((tm, D_MODEL), BF16)] * 2
        + [ext(ck[0])] * 4 + [ext(w) for w in ck[1:]] * 2,
        compiler_params=pltpu.CompilerParams(dimension_semantics=("arbitrary",),
                                             vmem_limit_bytes=FFN_PIPELINED_VMEM_LIMIT_BYTES),
    )(x, *mixer_args, gain, w_up, w_up, conv_w, conv_w, conv_b, conv_b, prev, prev, prev, prev, w_down)


def _state_to_halo(state, geom, hp):
    n, w1, ch = state.shape
    if geom.dil == 1:
        rows = state
    else:
        rows = state.transpose(1, 0, 2).reshape(1, w1 * n, ch)
    return jnp.pad(rows, ((0, 0), (hp - rows.shape[1], 0), (0, 0)))


def _halo_to_state(halo, geom, n, w1):
    ch = halo.shape[-1]
    if halo.shape[0] != geom.n_seq:
        halo = halo[geom.tps - 1::geom.tps]
    if geom.dil == 1:
        return halo[:, halo.shape[1] - w1:, :]
    return halo[0, halo.shape[1] - w1 * n:, :].reshape(w1, n, ch).transpose(1, 0, 2)


def _block_diag(w):
    nb, bi, bj = w.shape
    eye = jnp.eye(nb, dtype=w.dtype)
    return (eye[:, None, :, None] * w[:, :, None, :]).reshape(nb * bi, nb * bj)


def _trunk(x_rows, geom, n, pos0, p, paged, rg_h, rg_conv, pool_buf, ffn_buf, ffn_ck):
    depth = p["norm_mix"].shape[0]
    ks, vs, lfs, hs, cs, pbs, fbs = [], [], [], [], [], [], []
    dil = geom.dil
    t_steps = geom.rows // n
    dilp = _round_up(dil, SUBLANES)
    x = x_rows
    for layer in range(depth):
        li = layer // 2
        if layer % 2 == 0:
            w_in = p["ab_w_in"][li]
            a3 = 3 * A_WIDTH
            w = {
                "g": p["norm_mix"][layer][None, :],
                "wqk": w_in[:, :2 * A_WIDTH].astype(BF16),
                "wv": w_in[:, 2 * A_WIDTH:a3].astype(BF16),
                "wf": jnp.pad(w_in[:, a3:a3 + FOX_HEADS], ((0, 0), (0, F_PAD - FOX_HEADS))).astype(BF16),
                "wrg": w_in[:, a3 + FOX_HEADS:].astype(BF16),
                "bf": jnp.pad(p["ab_b_f"][li], (0, F_PAD - FOX_HEADS))[None, :],
                "qg": jnp.tile(p["ab_q_gain"][li], FOX_HEADS)[None, :],
                "kg": jnp.tile(p["ab_k_gain"][li], FOX_HEADS)[None, :],
                "bd": _block_diag(jnp.full((MXU_TILE // HEAD_DIM, HEAD_DIM, HEAD_DIM), 1.0 / HEAD_DIM,
                                           F32)).astype(BF16),
                "cw": p["ab_conv_w"][li], "cb": p["ab_conv_b"][li][None, :],
                "wa": _block_diag(p["ab_w_a"][li]).astype(BF16), "ba": p["ab_b_a"][li][None, :],
                "wx": _block_diag(p["ab_w_x"][li]).astype(BF16), "bx": p["ab_b_x"][li][None, :],
                "lam": p["ab_lambda"][li][None, :],
            }
            hp = geom.halo(RG_CONV)
            cprev = _state_to_halo(rg_conv[li], geom, hp)
            if dil == 1:
                h0 = jnp.pad(rg_h[li][:, None, :], ((0, 0), (0, dilp - 1), (0, 0)))
            else:
                h0 = jnp.pad(rg_h[li][None], ((0, 0), (0, dilp - dil), (0, 0)))
            scale = HEAD_DIM ** -0.5
            q_mult = scale * LOG2E if paged is None else scale
            q, k, v, kt, vt, lft, yrg, cst, hl = _ab_in_call(x, geom, w, cprev, h0, q_mult)

            if paged is None:
                shift = (SCORE_BOUND_SLACK * HEAD_DIM * q_mult * jnp.max(jnp.abs(p["ab_q_gain"][li]))
                         * jnp.max(jnp.abs(p["ab_k_gain"][li])) + 0.5)
                bounded = (shift < MAX_SAFE_SHIFT).astype(jnp.int32).reshape(1)
                cq, ck = _cumsum_call(lft.reshape(n * FOX_HEADS, t_steps), shift.reshape(1, 1))
                cq = cq.reshape(n, FOX_HEADS, GATE_ROWS, t_steps)
                ck = ck.reshape(n, FOX_HEADS, GATE_ROWS, t_steps)
                heads_t = lambda z: z.reshape(n, FOX_HEADS, HEAD_DIM, t_steps)
                attn = _attn_prompt_call(bounded, heads_t(q), cq, heads_t(k), ck, heads_t(v),
                                         min(512, t_steps))
                attn = attn.reshape(n, A_WIDTH, t_steps)
                ks.append(kt.reshape(n, FOX_HEADS, HEAD_DIM, t_steps).transpose(0, 3, 1, 2))
                vs.append(vt.reshape(n, FOX_HEADS, HEAD_DIM, t_steps).transpose(0, 3, 1, 2))
                lfs.append(lft.transpose(0, 2, 1))
                hs.append(hl[:, 0, :])
            else:
                ckt, cvt, clft, page_table = paged

                def heads_bm(z):
                    return z.reshape(t_steps, n, FOX_HEADS, HEAD_DIM).transpose(1, 2, 0, 3)
                lf_new = lft.transpose(2, 1, 0)[:, :, None, :]
                attn = _attn_sample_call(page_table, heads_bm(q), heads_bm(k), heads_bm(v), lf_new,
                                         ckt, cvt, clft, li)
                attn = attn.transpose(2, 0, 1, 3).reshape(geom.rows, A_WIDTH)
                ks.append(kt.reshape(t_steps, FOX_HEADS, HEAD_DIM, n).transpose(3, 0, 1, 2))
                vs.append(vt.reshape(t_steps, FOX_HEADS, HEAD_DIM, n).transpose(3, 0, 1, 2))
                lfs.append(lft.transpose(2, 0, 1))
                hs.append(hl[0, :dil, :])
            cs.append(_halo_to_state(cst, geom, n, RG_CONV - 1))
            w_out = p["ab_w_out"][li].astype(BF16)
            mixer_out = (attn, yrg, w_out[:A_WIDTH], w_out[A_WIDTH:])
        else:
            mixer_out = None
            hp = geom.halo(POOL_BUF + 1)
            prev = _state_to_halo(pool_buf[li], geom, hp)
            x, st = _pool_call(x, geom, p["norm_mix"][layer][None, :], prev,
                               p["pool_w"][li].astype(BF16), p["pool_scale"][li][None, :], pos0)
            pbs.append(_halo_to_state(st, geom, n, POOL_BUF))
        hp = geom.halo(FFN_CONV)
        prev = _state_to_halo(ffn_buf[layer], geom, hp)
        ffn_args = (x, geom, p["norm_ffn"][layer][None, :], p["ffn_w_up"][layer].astype(BF16),
                    p["ffn_conv_w"][layer], p["ffn_conv_b"][layer][None, :], prev,
                    (0.5 * p["ffn_w_down"][layer]).astype(BF16))
        block, ck = ffn_ck
        if block == D_FF and geom.n_tiles > 1 and dil == 1:
            x, sg, sv = _ffn_pipelined_call(*ffn_args, ck, mixer_out=mixer_out)
        else:
            x, sg, sv = _ffn_call(*ffn_args, block, ck, mixer_out=mixer_out)
        fbs.append(_halo_to_state(jnp.concatenate([sg, sv], axis=-1), geom, n, FFN_CONV - 1))
    return x, (jnp.stack(ks), jnp.stack(vs), jnp.stack(lfs), jnp.stack(hs), jnp.stack(cs),
               jnp.stack(pbs), jnp.stack(fbs))


def kernel(x_prompt, x_sample, cache_k, cache_v, cache_logf, state_rg_h, state_rg_conv, state_pool, state_ffn_conv, page_table, norm_mix, norm_ffn, ab_w_in, ab_b_f, ab_q_gain, ab_k_gain, ab_conv_w, ab_conv_b, ab_w_a, ab_b_a, ab_w_x, ab_b_x, ab_lambda, ab_w_out, pool_w, pool_scale, ffn_w_up, ffn_conv_w, ffn_conv_b, ffn_w_down):
    p = {
        "norm_mix": norm_mix, "norm_ffn": norm_ffn,
        "ab_w_in": ab_w_in, "ab_b_f": ab_b_f, "ab_q_gain": ab_q_gain, "ab_k_gain": ab_k_gain,
        "ab_conv_w": ab_conv_w, "ab_conv_b": ab_conv_b, "ab_w_a": ab_w_a, "ab_b_a": ab_b_a,
        "ab_w_x": ab_w_x, "ab_b_x": ab_b_x, "ab_lambda": ab_lambda, "ab_w_out": ab_w_out,
        "pool_w": pool_w, "pool_scale": pool_scale,
        "ffn_w_up": ffn_w_up, "ffn_conv_w": ffn_conv_w, "ffn_conv_b": ffn_conv_b, "ffn_w_down": ffn_w_down,
    }
    depth = norm_mix.shape[0]
    n_ab, n_pool = (depth + 1) // 2, depth // 2

    bsz, t, _ = x_prompt.shape
    geom_p = _Geom(bsz, t, 1, 512)
    y_p, st_p = _trunk(
        x_prompt.reshape(bsz * t, D_MODEL), geom_p, bsz, 0, p, None,
        jnp.zeros((n_ab, bsz, B_WIDTH), F32), jnp.zeros((n_ab, bsz, RG_CONV - 1, B_WIDTH), F32),
        jnp.zeros((n_pool, bsz, POOL_BUF, D_MODEL), F32),
        jnp.zeros((depth, bsz, FFN_CONV - 1, 2 * D_FF), F32), (D_FF, (1024, 1024, 1024)))
    y_prompt = y_p.reshape(bsz, t, D_MODEL)

    db, dt, _ = x_sample.shape
    n_pool_pages, page = cache_k.shape[1], cache_k.shape[2]
    past_len = page_table.shape[1] * page
    geom_s = _Geom(1, dt * db, db, 512)
    paged = (cache_k.transpose(0, 1, 3, 4, 2), cache_v.transpose(0, 1, 3, 4, 2),
             cache_logf.transpose(0, 1, 3, 2), page_table)
    y_s, st_s = _trunk(
        x_sample.transpose(1, 0, 2).reshape(dt * db, D_MODEL), geom_s, db, past_len, p, paged,
        state_rg_h, state_rg_conv, state_pool, state_ffn_conv, (512, (512,)))
    y_sample = y_s.reshape(dt, db, D_MODEL).transpose(1, 0, 2)
    return (y_prompt, y_sample) + st_p + st_s
```

```python
import functools

import jax
import jax.numpy as jnp
from jax import lax
from jax.experimental import pallas as pl
from jax.experimental.pallas import tpu as pltpu

D_MODEL = 1024
A_WIDTH = 512
B_WIDTH = 512
HEAD_DIM = 64
FOX_HEADS = 8
RG_CONV = 4
RG_C = 8.0
POOL_WINDOWS = (2, 4, 8, 16)
POOL_GD = 256
POOL_BUF = 15
D_FF = 3072
FFN_CONV = 3
EPS = 1e-6
NEG = -1e30
F_PAD = 128
GATE_ROWS = 16
ATTN_HEADS_PER_STEP = 2
DECODE_SEQS_PER_STEP = 2
SCORE_BOUND_SLACK = 1.02
MAX_SAFE_SHIFT = 56.0
LOG2E = 1.4426950408889634
SUBLANES = 8
MXU_TILE = 256
F32 = jnp.float32
BF16 = jnp.bfloat16
VMEM_LIMIT_BYTES = 56 * 1024 * 1024
FFN_PIPELINED_VMEM_LIMIT_BYTES = 62 * 1024 * 1024


def _round_up(x, m):
    return -(-x // m) * m


def _full_spec(shape):
    return pl.BlockSpec(shape, lambda *_: (0,) * len(shape))


def _params(n_axes):
    return pltpu.CompilerParams(dimension_semantics=("arbitrary",) * n_axes,
                                vmem_limit_bytes=VMEM_LIMIT_BYTES)


def _dot(a, b):
    return jnp.dot(a, b, preferred_element_type=F32)


def _dot_nt(a, b):
    return lax.dot_general(a, b, (((1,), (1,)), ((), ())), preferred_element_type=F32)


def _rms(x, gain):
    y = x * lax.rsqrt(jnp.mean(x * x, axis=-1, keepdims=True) + EPS)
    return y * gain


def _gelu(x):
    return 0.5 * x * (1.0 + jnp.tanh(0.7978845608028654 * (x + 0.044715 * (x * x * x))))


def _gelu_doubled(x):
    c = 0.7978845608028654
    return x * (1.0 + jnp.tanh(x * (c + (c * 0.044715) * (x * x))))


def _softplus(x):
    return jnp.maximum(x, 0.0) + jnp.log1p(jnp.exp(-jnp.abs(x)))


class _Geom:
    def __init__(self, n_seq, rows_per_seq, dil, tile):
        self.n_seq, self.rows_per_seq, self.dil = n_seq, rows_per_seq, dil
        self.tm = min(tile, rows_per_seq)
        assert rows_per_seq % self.tm == 0 and self.tm % dil == 0 and self.tm % SUBLANES == 0
        assert dil == 1 or dil % SUBLANES == 0
        self.tps = rows_per_seq // self.tm
        self.rows = n_seq * rows_per_seq
        self.n_tiles = self.rows // self.tm

    def halo(self, width):
        return _round_up((width - 1) * self.dil, SUBLANES)


def _conv_from_ext(ext_ref, w_ref, bias, width, dil, tm, hp):
    y = bias
    for j in range(width):
        off = hp - (width - 1 - j) * dil
        y = y + ext_ref[off:off + tm, :] * w_ref[j:j + 1, :]
    return y


def _ab_in_kernel(x_ref, g_ref, wqk_ref, wv_ref, wrg_ref, wf_ref, bf_ref, qg_ref, kg_ref, bd_ref,
                  cw_ref, cb_ref, wa_ref, ba_ref, wx_ref, bx_ref, lam_ref, cprev_ref, h0_ref,
                  q_out, k_out, v_out, kt_out, vt_out, lft_out, y_out, cst_out, h_out,
                  ext_ref, hc_ref, *, tm, tps, dil, hp, q_mult):
    i = pl.program_id(0)
    first = (i % tps) == 0
    xn = _rms(x_ref[...], g_ref[...]).astype(BF16)

    bd = bd_ref[...]

    def head_norm(z, gain):
        z2 = (z * z).astype(BF16)
        wide = bd.shape[0]
        ms = jnp.concatenate([_dot(z2[:, c:c + wide], bd) for c in range(0, A_WIDTH, wide)], axis=1)
        return z * lax.rsqrt(ms + EPS) * gain

    qk = _dot(xn, wqk_ref[...])
    q = head_norm(qk[:, :A_WIDTH], qg_ref[...])
    k = head_norm(qk[:, A_WIDTH:], kg_ref[...])
    v = _dot(xn, wv_ref[...])
    q = q * q_mult

    f = _dot(xn, wf_ref[...]) + bf_ref[...]
    lf = jnp.minimum(f, 0.0) - jnp.log1p(jnp.exp(-jnp.abs(f)))

    unit = tm if dil == 1 else dil
    for u in range(tm // unit):
        rows_u = slice(u * unit, (u + 1) * unit)
        kt = k[rows_u, :].T
        vt = v[rows_u, :].T
        kt_out[u] = kt
        vt_out[u] = vt
        lft_out[u] = lf[rows_u, :].T[:FOX_HEADS, :]
        if dil == 1:
            q_out[u] = q.T.astype(BF16)
            k_out[u] = kt.astype(BF16)
            v_out[u] = vt.astype(BF16)
    if dil != 1:
        q_out[...] = q.astype(BF16)
        k_out[...] = k.astype(BF16)
        v_out[...] = v.astype(BF16)

    rg = _dot(xn, wrg_ref[...])
    xr = rg[:, :B_WIDTH]
    gate = rg[:, B_WIDTH:]

    @pl.when(first)
    def _():
        ext_ref[0:hp, :] = cprev_ref[0]
        hc_ref[...] = h0_ref[0]

    ext_ref[hp:hp + tm, :] = xr
    xc = _conv_from_ext(ext_ref, cw_ref, cb_ref[...], RG_CONV, dil, tm, hp)
    tail = ext_ref[tm:tm + hp, :]
    ext_ref[0:hp, :] = tail
    cst_out[0] = tail

    xcb = xc.astype(BF16)
    r = jax.nn.sigmoid(_dot(xcb, wa_ref[...]) + ba_ref[...])
    gi = jax.nn.sigmoid(_dot(xcb, wx_ref[...]) + bx_ref[...])
    log_a = -RG_C * r * _softplus(-lam_ref[...])
    a = jnp.exp(log_a)
    inp = jnp.sqrt(-jnp.tanh(log_a) * (a * a + 1.0)) * gi * xc

    unit = max(dil, SUBLANES)
    sub = lax.broadcasted_iota(jnp.int32, (tm, 1), 0) % unit
    s = dil
    while s < unit:
        valid = sub >= s
        a_sh = jnp.where(valid, pltpu.roll(a, s, 0), 1.0)
        h_sh = jnp.where(valid, pltpu.roll(inp, s, 0), 0.0)
        inp = a * h_sh + inp
        a = a * a_sh
        s *= 2

    carry = hc_ref[0:dil, :]
    groups = []
    for g in range(tm // unit):
        rows_g = slice(g * unit, (g + 1) * unit)
        incoming = jnp.broadcast_to(carry, (unit, B_WIDTH)) if dil < unit else carry
        hg = inp[rows_g, :] + a[rows_g, :] * incoming
        groups.append(hg)
        carry = hg[unit - dil:unit, :]
    h = jnp.concatenate(groups, axis=0)
    hc_ref[0:dil, :] = carry
    h_out[0] = hc_ref[...]
    y_out[...] = (h * _gelu(gate)).astype(BF16)


def _ab_in_call(x, geom, w, cprev, h0, q_mult):
    tm, hp, tps = geom.tm, geom.halo(RG_CONV), geom.tps
    dilp = _round_up(geom.dil, SUBLANES)
    rows = geom.rows
    tile = lambda n: pl.BlockSpec((tm, n), lambda i: (i, 0))
    seq3 = lambda r, n: pl.BlockSpec((1, r, n), lambda i: (i // tps, 0, 0))
    if geom.dil == 1:
        t_shape = lambda ch: (geom.n_seq, ch, geom.rows_per_seq)
        t_spec = lambda ch: pl.BlockSpec((1, ch, tm), lambda i: (i // tps, 0, i % tps))
        op_spec, op_shape = t_spec(A_WIDTH), jax.ShapeDtypeStruct(t_shape(A_WIDTH), BF16)
    else:
        t_shape = lambda ch: (rows // geom.dil, ch, geom.dil)
        t_spec = lambda ch: pl.BlockSpec((tm // geom.dil, ch, geom.dil), lambda i: (i, 0, 0))
        op_spec, op_shape = tile(A_WIDTH), jax.ShapeDtypeStruct((rows, A_WIDTH), BF16)
    consts = [w["g"], w["wqk"], w["wv"], w["wrg"], w["wf"], w["bf"], w["qg"], w["kg"], w["bd"],
              w["cw"], w["cb"], w["wa"], w["ba"], w["wx"], w["bx"], w["lam"]]
    kern = functools.partial(_ab_in_kernel, tm=tm, tps=tps, dil=geom.dil, hp=hp, q_mult=q_mult)
    return pl.pallas_call(
        kern,
        grid=(geom.n_tiles,),
        in_specs=[tile(D_MODEL)] + [_full_spec(c.shape) for c in consts]
        + [seq3(hp, B_WIDTH), seq3(dilp, B_WIDTH)],
        out_specs=[op_spec, op_spec, op_spec,
                   t_spec(A_WIDTH), t_spec(A_WIDTH), t_spec(FOX_HEADS), tile(B_WIDTH),
                   seq3(hp, B_WIDTH), seq3(dilp, B_WIDTH)],
        out_shape=[op_shape, op_shape, op_shape,
                   jax.ShapeDtypeStruct(t_shape(A_WIDTH), F32),
                   jax.ShapeDtypeStruct(t_shape(A_WIDTH), F32),
                   jax.ShapeDtypeStruct(t_shape(FOX_HEADS), F32),
                   jax.ShapeDtypeStruct((rows, B_WIDTH), BF16),
                   jax.ShapeDtypeStruct((geom.n_seq, hp, B_WIDTH), F32),
                   jax.ShapeDtypeStruct((geom.n_seq, dilp, B_WIDTH), F32)],
        scratch_shapes=[pltpu.VMEM((hp + tm, B_WIDTH), F32), pltpu.VMEM((dilp, B_WIDTH), F32)],
        compiler_params=_params(1),
    )(x, *consts, cprev, h0)


def _cumsum_kernel(x_ref, shift_ref, cq_ref, ck_ref, *, n):
    x = x_ref[...]
    lane = lax.broadcasted_iota(jnp.int32, x.shape, 1)
    s = 1
    while s < n:
        x = x + jnp.where(lane >= s, pltpu.roll(x, s, 1), 0.0)
        s *= 2
    x = x * LOG2E

    def split(v):
        hi = v.astype(BF16).astype(F32)
        rest = v - hi
        mid = rest.astype(BF16).astype(F32)
        return hi, mid, rest - mid

    q_parts, k_parts = split(x - shift_ref[...]), split(x)
    row = lax.broadcasted_iota(jnp.int32, (GATE_ROWS, n), 0)
    for r in range(x.shape[0]):
        cq = jnp.where(row < 3, -1.0, 0.0)
        ck = jnp.where((row >= 3) & (row < 6), 1.0, 0.0)
        for j in range(3):
            cq = jnp.where(row == 3 + j, jnp.broadcast_to(q_parts[j][r:r + 1, :], (GATE_ROWS, n)), cq)
            ck = jnp.where(row == j, jnp.broadcast_to(k_parts[j][r:r + 1, :], (GATE_ROWS, n)), ck)
        cq_ref[r] = cq.astype(BF16)
        ck_ref[r] = ck.astype(BF16)


def _cumsum_call(x, shift):
    rows, n = x.shape
    out = jax.ShapeDtypeStruct((rows, GATE_ROWS, n), BF16)
    return pl.pallas_call(
        functools.partial(_cumsum_kernel, n=n),
        grid=(1,),
        in_specs=[_full_spec(x.shape), _full_spec(shift.shape)],
        out_specs=[_full_spec(out.shape)] * 2,
        out_shape=[out] * 2,
        compiler_params=_params(1),
    )(x, shift)


def _attn_prompt_kernel(bounded_ref, qt_ref, cq_ref, kt_ref, ck_ref, vt_ref, o_ref, *, tq, sb, depth):
    for online in (False, True):
        @pl.when((bounded_ref[0] == 0) == online)
        def _(online=online):
            _attn_prompt_body(qt_ref, cq_ref, kt_ref, ck_ref, vt_ref, o_ref,
                              tq=tq, sb=sb, depth=depth, online=online)


def _attn_prompt_body(qt_ref, cq_ref, kt_ref, ck_ref, vt_ref, o_ref, *, tq, sb, depth, online):
    qi = pl.program_id(2)
    n_heads = qt_ref.shape[1]
    heads = range(n_heads)
    qt = [jnp.concatenate([qt_ref[0, h], cq_ref[0, h]], axis=0) for h in heads]
    n_sb = tq // sb
    row = lax.broadcasted_iota(jnp.int32, (sb, tq), 0)
    col = lax.broadcasted_iota(jnp.int32, (sb, tq), 1)

    def key_rows(j, s):
        return pl.ds(pl.multiple_of(j * tq + s * sb, sb), sb)

    def scores(h, j, s):
        kt = jnp.concatenate([kt_ref[0, h, :, key_rows(j, s)], ck_ref[0, h, :, key_rows(j, s)]], axis=0)
        return lax.dot_general(kt, qt[h], (((0,), (0,)), ((), ())), preferred_element_type=F32)

    def chunk(j, carry, diagonal):
        state, ahead = list(carry[0]), [list(a) for a in carry[1]]
        for s in range(n_sb):
            for h in heads:
                m, l, acc = state[h]
                st = ahead[h].pop(0)
                if s + depth < n_sb:
                    ahead[h].append(scores(h, j, s + depth))
                elif not diagonal:
                    ahead[h].append(scores(h, j + 1, s + depth - n_sb))
                if diagonal:
                    st = jnp.where(row + s * sb <= col, st, NEG)
                if online:
                    m_new = jnp.maximum(m, jnp.max(st, axis=0, keepdims=True))
                    alpha = jnp.exp2(m - m_new)
                    p = jnp.exp2(st - m_new)
                    l = alpha * l + jnp.sum(p, axis=0, keepdims=True)
                    acc = alpha * acc + _dot(vt_ref[0, h, :, key_rows(j, s)], p.astype(BF16))
                    m = m_new
                else:
                    p = jnp.exp2(st)
                    l = l + jnp.sum(p, axis=0, keepdims=True)
                    acc = acc + _dot(vt_ref[0, h, :, key_rows(j, s)], p.astype(BF16))
                state[h] = (m, l, acc)
        return tuple(state), tuple(tuple(a) for a in ahead)

    state = tuple((jnp.full((1, tq), NEG, F32), jnp.zeros((1, tq), F32), jnp.zeros((HEAD_DIM, tq), F32))
                  for _ in heads)
    carry = (state, tuple(tuple(scores(h, 0, s) for s in range(depth)) for h in heads))
    carry = lax.fori_loop(0, qi, lambda j, c: chunk(j, c, False), carry)
    state, _ = chunk(qi, carry, True)
    for h, (_, l, acc) in enumerate(state):
        o_ref[0, h] = (acc / l).astype(BF16)


def _attn_prompt_call(bounded, q_t, cq, k_t, ck, v_t, tq):
    bsz, nh, dh, t = q_t.shape
    sb = min(128, tq)
    depth = min(2, tq // sb)
    hg = ATTN_HEADS_PER_STEP
    tile = lambda r: pl.BlockSpec((1, hg, r, tq), lambda b, h, i, flag: (b, h, 0, i))
    whole = lambda r: pl.BlockSpec((1, hg, r, t), lambda b, h, i, flag: (b, h, 0, 0))
    grid_spec = pltpu.PrefetchScalarGridSpec(
        num_scalar_prefetch=1, grid=(bsz, nh // hg, t // tq),
        in_specs=[tile(dh), tile(GATE_ROWS), whole(dh), whole(GATE_ROWS), whole(dh)],
        out_specs=tile(dh))
    return pl.pallas_call(
        functools.partial(_attn_prompt_kernel, tq=tq, sb=sb, depth=depth),
        grid_spec=grid_spec,
        out_shape=jax.ShapeDtypeStruct((bsz, nh, dh, t), BF16),
        compiler_params=_params(3),
    )(bounded, q_t, cq, k_t, ck, v_t)


def _attn_sample_kernel(pt_ref, q_ref, kn_ref, vn_ref, lfn_ref, *refs, n_pages, page, dt, n_seq):
    del pt_ref
    o_ref = refs[3 * n_seq * n_pages]
    for sq in range(n_seq):
        pages = lambda kind: refs[(kind * n_seq + sq) * n_pages:(kind * n_seq + sq + 1) * n_pages]
        o_ref[sq] = _attn_sample_one(q_ref[sq], kn_ref[sq], vn_ref[sq], lfn_ref[sq],
                                     pages(0), pages(1), pages(2), n_pages=n_pages, page=page, dt=dt)


def _attn_sample_one(q3, kn, vn, lfn, k_refs, v_refs, lf_refs, *, n_pages, page, dt):
    nh = FOX_HEADS
    past = n_pages * page
    bdot = lambda a, b: lax.dot_general(a, b, (((2,), (1,)), ((0,), (0,))), preferred_element_type=F32)
    bdot_nt = lambda a, b: lax.dot_general(a, b, (((2,), (2,)), ((0,), (0,))), preferred_element_type=F32)

    lft = jnp.concatenate([r[0, 0] for r in lf_refs], axis=1)
    lane = lax.broadcasted_iota(jnp.int32, lft.shape, 1)
    suf = lft
    s = 1
    while s < past:
        suf = suf + jnp.where(lane < past - s, pltpu.roll(suf, past - s, 1), 0.0)
        s *= 2
    suf = suf - lft
    suf3 = jnp.stack([jnp.broadcast_to(suf[h:h + 1, :], (dt, past)) for h in range(nh)], axis=0)

    lnew = jnp.broadcast_to(lfn, (nh, dt, dt))
    colq = lax.broadcasted_iota(jnp.int32, (nh, dt, dt), 2)
    tq = lax.broadcasted_iota(jnp.int32, (nh, dt, dt), 1)
    causal = colq <= tq
    nq = jnp.sum(jnp.where(causal, lnew, 0.0), axis=2, keepdims=True)
    g = jnp.zeros((nh, dt, dt), F32)
    for l in range(dt):
        g = g + jnp.where(colq >= l, lnew[:, :, l:l + 1], 0.0)

    kt_all = jnp.concatenate([r[0, 0].astype(BF16) for r in k_refs], axis=2)
    vt_all = jnp.concatenate([r[0, 0].astype(BF16) for r in v_refs], axis=2)
    s_past = bdot(q3, kt_all) + (suf3 + nq)
    s_new = jnp.where(causal, bdot_nt(q3, kn) + (nq - g), NEG)

    m = jnp.maximum(jnp.max(s_new, axis=2, keepdims=True), jnp.max(s_past, axis=2, keepdims=True))
    p_new = jnp.exp(s_new - m)
    p_past = jnp.exp(s_past - m)
    l = jnp.sum(p_new, axis=2, keepdims=True) + jnp.sum(p_past, axis=2, keepdims=True)
    acc = bdot(p_new.astype(BF16), vn) + bdot_nt(p_past.astype(BF16), vt_all)
    return (acc / l).astype(BF16)


def _attn_sample_call(page_table, q, k_new, v_new, lf_new, cache_kt, cache_vt, cache_lft, layer):
    db, nh, dt, dh = q.shape
    n_pages = page_table.shape[1]
    page = cache_kt.shape[-1]
    n_seq = DECODE_SEQS_PER_STEP if db % DECODE_SEQS_PER_STEP == 0 else 1
    new_spec = pl.BlockSpec((n_seq, nh, dt, dh), lambda b, pt: (b, 0, 0, 0))

    def page_specs(shape):
        zeros = (0,) * len(shape)
        return [pl.BlockSpec((1, 1) + shape, lambda b, pt, sq=sq, p=p: (layer, pt[b * n_seq + sq, p]) + zeros)
                for sq in range(n_seq) for p in range(n_pages)]

    in_specs = ([new_spec, new_spec, new_spec,
                 pl.BlockSpec((n_seq, nh, 1, dt), lambda b, pt: (b, 0, 0, 0))]
                + page_specs((nh, dh, page)) + page_specs((nh, dh, page)) + page_specs((nh, page)))
    grid_spec = pltpu.PrefetchScalarGridSpec(
        num_scalar_prefetch=1, grid=(db // n_seq,), in_specs=in_specs, out_specs=new_spec)
    per_kind = n_seq * n_pages
    return pl.pallas_call(
        functools.partial(_attn_sample_kernel, n_pages=n_pages, page=page, dt=dt, n_seq=n_seq),
        grid_spec=grid_spec,
        out_shape=jax.ShapeDtypeStruct((db, nh, dt, dh), BF16),
        compiler_params=_params(1),
    )(page_table, q, k_new, v_new, lf_new,
      *([cache_kt] * per_kind), *([cache_vt] * per_kind), *([cache_lft] * per_kind))


def _pool_kernel(x_ref, gain_ref, prev_ref, w_ref, scale_ref, o_ref, st_ref, ext_ref,
                 *, tm, tps, dil, hp, pos0):
    i = pl.program_id(0)
    first = (i % tps) == 0
    x = x_ref[...]
    xn = _rms(x, gain_ref[...])
    rows = lax.broadcasted_iota(jnp.int32, (tm, 1), 0) + (i % tps) * tm
    pos = lax.div(rows, jnp.int32(dil)) + pos0

    @pl.when(first)
    def _():
        ext_ref[0:hp, :] = prev_ref[0]

    ext_ref[hp:hp + tm, :] = xn
    for gi, window in enumerate(POOL_WINDOWS):
        cols = slice(gi * POOL_GD, (gi + 1) * POOL_GD)
        xg = xn[:, cols]
        total = xg
        for j in range(1, window):
            total = total + ext_ref[hp - j * dil:hp - j * dil + tm, cols]
        cnt = jnp.minimum(pos + 1, window).astype(F32)
        diff = total / cnt - xg
        y = _dot(diff.astype(BF16), w_ref[gi]) * scale_ref[:, cols]
        o_ref[:, cols] = x[:, cols] + y
    tail = ext_ref[tm:tm + hp, :]
    ext_ref[0:hp, :] = tail
    st_ref[0] = tail


def _pool_call(x, geom, gain, prev, w, scale, pos0):
    tm, hp = geom.tm, geom.halo(POOL_BUF + 1)
    kern = functools.partial(_pool_kernel, tm=tm, tps=geom.tps, dil=geom.dil, hp=hp, pos0=pos0)
    resident = {"pipeline_mode": pl.Buffered(1)} if geom.n_seq == 1 else {}
    return pl.pallas_call(
        kern,
        grid=(geom.n_tiles,),
        in_specs=[pl.BlockSpec((tm, D_MODEL), lambda i: (i, 0)),
                  _full_spec(gain.shape),
                  pl.BlockSpec((1, hp, D_MODEL), lambda i: (i // geom.tps, 0, 0), **resident),
                  _full_spec(w.shape),
                  _full_spec(scale.shape)],
        out_specs=[pl.BlockSpec((tm, D_MODEL), lambda i: (i, 0)),
                   pl.BlockSpec((1, hp, D_MODEL), lambda i: (i, 0, 0))],
        out_shape=[jax.ShapeDtypeStruct((geom.rows, D_MODEL), F32),
                   jax.ShapeDtypeStruct((geom.n_tiles, hp, D_MODEL), F32)],
        scratch_shapes=[pltpu.VMEM((hp + tm, D_MODEL), F32)],
        compiler_params=_params(1),
    )(x, gain, prev, w, scale)


def _ffn_kernel(x_ref, *refs, tm, tps, dil, hp, ck, mixer):
    if mixer is not None:
        a_ref, y_ref, wa_ref, wy_ref = refs[:4]
        refs = refs[4:]
    (g_ref, wug_ref, wuv_ref, cwg_ref, cwv_ref, cbg_ref, cbv_ref, pg_ref, pv_ref, wd_ref,
     o_ref, sg_ref, sv_ref, xn_ref, carg_ref, carv_ref) = refs[:16]
    ext_refs = refs[16:]
    i = pl.program_id(0)
    c = pl.program_id(1)
    first = (i % tps) == 0
    n_sub = len(ext_refs) // 2
    extg, extv = ext_refs[:n_sub], ext_refs[n_sub:]
    starts = [sum(ck[:s]) for s in range(n_sub)]
    cols = lambda s: slice(starts[s], starts[s] + ck[s])

    @pl.when(c == 0)
    def _():
        x = x_ref[...]
        if mixer == "transposed":
            x = x + lax.dot_general(a_ref[0], wa_ref[...], (((0,), (0,)), ((), ())),
                                    preferred_element_type=F32)
        elif mixer == "rows":
            x = x + _dot(a_ref[...], wa_ref[...])
        if mixer is not None:
            x = x + _dot(y_ref[...], wy_ref[...])
        xn_ref[...] = _rms(x, g_ref[...]).astype(BF16)
        o_ref[...] = x

    @pl.when(first)
    def _():
        for s in range(n_sub):
            extg[s][0:hp, :] = pg_ref[0, :, cols(s)]
            extv[s][0:hp, :] = pv_ref[0, :, cols(s)]

    @pl.when(jnp.logical_not(first))
    def _():
        for s in range(n_sub):
            extg[s][0:hp, :] = carg_ref[c, :, cols(s)]
            extv[s][0:hp, :] = carv_ref[c, :, cols(s)]

    xn = xn_ref[...]

    def up(s):
        extg[s][hp:hp + tm, :] = _dot(xn, wug_ref[:, cols(s)])
        extv[s][hp:hp + tm, :] = _dot(xn, wuv_ref[:, cols(s)])

    def gated(s):
        yg = _conv_from_ext(extg[s], cwg_ref[:, cols(s)], cbg_ref[:, cols(s)], FFN_CONV, dil, tm, hp)
        yv = _conv_from_ext(extv[s], cwv_ref[:, cols(s)], cbv_ref[:, cols(s)], FFN_CONV, dil, tm, hp)
        return (_gelu_doubled(yg) * yv).astype(BF16)

    ahead = min(2, n_sub)
    for s in range(ahead):
        up(s)
    acc = None
    for s in range(n_sub):
        if s + ahead < n_sub:
            up(s + ahead)
        down = _dot(gated(s), wd_ref[cols(s), :])
        acc = down if acc is None else acc + down
        for ext, car_ref, s_ref in ((extg[s], carg_ref, sg_ref), (extv[s], carv_ref, sv_ref)):
            tail = ext[tm:tm + hp, :]
            car_ref[c, :, cols(s)] = tail
            s_ref[0, :, cols(s)] = tail
    o_ref[...] += acc


def _ffn_call(x, geom, gain, w_up, conv_w, conv_b, prev, w_down, block, ck, mixer_out=None):
    tm, hp = geom.tm, geom.halo(FFN_CONV)
    nblk = D_FF // block
    assert sum(ck) == block
    tps = geom.tps
    resident = {"pipeline_mode": pl.Buffered(1)} if nblk == 1 else {}
    col = lambda r, off, **kw: pl.BlockSpec((r, block), lambda i, c: (0, off + c), **kw)
    st_in = lambda off: pl.BlockSpec((1, hp, block), lambda i, c: (i // tps, 0, off + c))
    st_out = pl.BlockSpec((1, hp, block), lambda i, c: (i, 0, c))
    rows_spec = lambda n: pl.BlockSpec((tm, n), lambda i, c: (i, 0))
    mixer, mixer_args, mixer_specs = None, [], []
    if mixer_out is not None:
        attn, yrg, w_attn, w_rg = mixer_out
        mixer = "transposed" if attn.ndim == 3 else "rows"
        a_spec = (pl.BlockSpec((1, A_WIDTH, tm), lambda i, c: (i // tps, 0, i % tps))
                  if mixer == "transposed" else rows_spec(A_WIDTH))
        w_spec = lambda w: pl.BlockSpec(w.shape, lambda i, c: (0, 0), pipeline_mode=pl.Buffered(1))
        mixer_args = [attn, yrg, w_attn, w_rg]
        mixer_specs = [a_spec, rows_spec(B_WIDTH), w_spec(w_attn), w_spec(w_rg)]
    kern = functools.partial(_ffn_kernel, tm=tm, tps=tps, dil=geom.dil, hp=hp, ck=ck, mixer=mixer)
    return pl.pallas_call(
        kern,
        grid=(geom.n_tiles, nblk),
        in_specs=[rows_spec(D_MODEL)] + mixer_specs + [
                  _full_spec(gain.shape),
                  col(D_MODEL, 0, **resident), col(D_MODEL, nblk, **resident),
                  col(FFN_CONV, 0), col(FFN_CONV, nblk),
                  col(1, 0), col(1, nblk),
                  st_in(0), st_in(nblk),
                  pl.BlockSpec((block, D_MODEL), lambda i, c: (c, 0), **resident)],
        out_specs=[pl.BlockSpec((tm, D_MODEL), lambda i, c: (i, 0)), st_out, st_out],
        out_shape=[jax.ShapeDtypeStruct((geom.rows, D_MODEL), F32),
                   jax.ShapeDtypeStruct((geom.n_tiles, hp, D_FF), F32),
                   jax.ShapeDtypeStruct((geom.n_tiles, hp, D_FF), F32)],
        scratch_shapes=[pltpu.VMEM((tm, D_MODEL), BF16),
                        pltpu.VMEM((nblk, hp, block), F32), pltpu.VMEM((nblk, hp, block), F32)]
        + [pltpu.VMEM((hp + tm, w), F32) for w in ck] * 2,
        compiler_params=_params(2),
    )(x, *mixer_args, gain, w_up, w_up, conv_w, conv_w, conv_b, conv_b, prev, prev, w_down)


def _ffn_pipelined_kernel(x_ref, *refs, tm, tps, n_tiles, dil, hp, ck, mixer):
    if mixer is not None:
        a_ref, y_ref, wa_ref, wy_ref = refs[:4]
        refs = refs[4:]
    (g_ref, wug_ref, wuv_ref, cwg_ref, cwv_ref, cbg_ref, cbv_ref, pgp_ref, pvp_ref, pgf_ref, pvf_ref,
     wd_ref, o_ref, sg_ref, sv_ref) = refs[:15]
    scratch = refs[15:]
    xres, xnb = scratch[0:2], scratch[2:4]
    ext0g, ext0v = scratch[4:6], scratch[6:8]
    n_sub = len(ck)
    extg = (None,) + tuple(scratch[8:8 + n_sub - 1])
    extv = (None,) + tuple(scratch[8 + n_sub - 1:8 + 2 * (n_sub - 1)])
    starts = [sum(ck[:s]) for s in range(n_sub)]
    cols = lambda s: slice(starts[s], starts[s] + ck[s])
    i = pl.program_id(0)
    tile_p = jnp.minimum(i, n_tiles - 1)
    tile_f = jnp.maximum(i - 1, 0)
    first_p = (tile_p % tps) == 0
    first_f = (tile_f % tps) == 0

    def prepare_norm(slot):
        x = x_ref[...]
        if mixer == "transposed":
            x = x + lax.dot_general(a_ref[0], wa_ref[...], (((0,), (0,)), ((), ())),
                                    preferred_element_type=F32)
        elif mixer == "rows":
            x = x + _dot(a_ref[...], wa_ref[...])
        if mixer is not None:
            x = x + _dot(y_ref[...], wy_ref[...])
        xres[slot][...] = x
        xnb[slot][...] = _rms(x, g_ref[...]).astype(BF16)

    def prepare_up(slot, halo_g, halo_v):
        xn = xnb[slot][...]
        ext0g[slot][0:hp, :] = halo_g
        ext0v[slot][0:hp, :] = halo_v
        ext0g[slot][hp:hp + tm, :] = _dot(xn, wug_ref[:, cols(0)])
        ext0v[slot][hp:hp + tm, :] = _dot(xn, wuv_ref[:, cols(0)])

    def finish_and_prepare(slot_f, slot_p):
        xn = xnb[slot_f][...]
        eg = (ext0g[slot_f],) + extg[1:]
        ev = (ext0v[slot_f],) + extv[1:]
        for s in range(1, n_sub):
            eg[s][0:hp, :] = jnp.where(first_f, pgf_ref[0, :, cols(s)], eg[s][tm:tm + hp, :])
            ev[s][0:hp, :] = jnp.where(first_f, pvf_ref[0, :, cols(s)], ev[s][tm:tm + hp, :])
        tail0g, tail0v = eg[0][tm:tm + hp, :], ev[0][tm:tm + hp, :]

        def up(s):
            eg[s][hp:hp + tm, :] = _dot(xn, wug_ref[:, cols(s)])
            ev[s][hp:hp + tm, :] = _dot(xn, wuv_ref[:, cols(s)])

        def gated(s):
            yg = _conv_from_ext(eg[s], cwg_ref[:, cols(s)], cbg_ref[:, cols(s)], FFN_CONV, dil, tm, hp)
            yv = _conv_from_ext(ev[s], cwv_ref[:, cols(s)], cbv_ref[:, cols(s)], FFN_CONV, dil, tm, hp)
            return (_gelu_doubled(yg) * yv).astype(BF16)

        prepare_norm(slot_p)
        acc = None
        for s in range(n_sub):
            if s + 1 < n_sub:
                up(s + 1)
            else:
                prepare_up(slot_p, jnp.where(first_p, pgp_ref[0, :, cols(0)], tail0g),
                           jnp.where(first_p, pvp_ref[0, :, cols(0)], tail0v))
            down = _dot(gated(s), wd_ref[cols(s), :])
            acc = down if acc is None else acc + down
        o_ref[...] = xres[slot_f][...] + acc
        sg_ref[0, :, cols(0)] = tail0g
        sv_ref[0, :, cols(0)] = tail0v
        for s in range(1, n_sub):
            sg_ref[0, :, cols(s)] = eg[s][tm:tm + hp, :]
            sv_ref[0, :, cols(s)] = ev[s][tm:tm + hp, :]

    @pl.when(i == 0)
    def _():
        prepare_norm(0)
        prepare_up(0, pgp_ref[0, :, cols(0)], pvp_ref[0, :, cols(0)])
        for s in range(1, n_sub):
            extg[s][tm:tm + hp, :] = jnp.zeros((hp, ck[s]), F32)
            extv[s][tm:tm + hp, :] = jnp.zeros((hp, ck[s]), F32)

    for parity in (0, 1):
        @pl.when((i > 0) & (i % 2 == parity))
        def _(parity=parity):
            finish_and_prepare(1 - parity, parity)


def _ffn_pipelined_call(x, geom, gain, w_up, conv_w, conv_b, prev, w_down, ck, mixer_out=None):
    tm, hp, tps, n_tiles = geom.tm, geom.halo(FFN_CONV), geom.tps, geom.n_tiles
    assert sum(ck) == D_FF and len(ck) >= 2 and geom.dil == 1
    tile_p = lambda i: jnp.minimum(i, n_tiles - 1)
    tile_f = lambda i: jnp.maximum(i - 1, 0)
    const = lambda a: pl.BlockSpec(a.shape, lambda i: (0,) * a.ndim, pipeline_mode=pl.Buffered(1))
    half = lambda r, off: pl.BlockSpec((r, D_FF), lambda i: (0, off), pipeline_mode=pl.Buffered(1))
    st_p = lambda off: pl.BlockSpec((1, hp, D_FF), lambda i: (tile_p(i) // tps, 0, off))
    st_f = lambda off: pl.BlockSpec((1, hp, D_FF), lambda i: (tile_f(i) // tps, 0, off))
    st_out = pl.BlockSpec((1, hp, D_FF), lambda i: (tile_f(i), 0, 0))
    rows_p = lambda n: pl.BlockSpec((tm, n), lambda i: (tile_p(i), 0))
    mixer, mixer_args, mixer_specs = None, [], []
    if mixer_out is not None:
        attn, yrg, w_attn, w_rg = mixer_out
        mixer = "transposed" if attn.ndim == 3 else "rows"
        a_spec = (pl.BlockSpec((1, A_WIDTH, tm), lambda i: (tile_p(i) // tps, 0, tile_p(i) % tps))
                  if mixer == "transposed" else rows_p(A_WIDTH))
        mixer_args = [attn, yrg, w_attn, w_rg]
        mixer_specs = [a_spec, rows_p(B_WIDTH), const(w_attn), const(w_rg)]
    kern = functools.partial(_ffn_pipelined_kernel, tm=tm, tps=tps, n_tiles=n_tiles, dil=geom.dil,
                             hp=hp, ck=ck, mixer=mixer)
    ext = lambda w: pltpu.VMEM((hp + tm, w), F32)
    return pl.pallas_call(
        kern,
        grid=(n_tiles + 1,),
        in_specs=[rows_p(D_MODEL)] + mixer_specs + [
                  const(gain),
                  half(D_MODEL, 0), half(D_MODEL, 1),
                  half(FFN_CONV, 0), half(FFN_CONV, 1),
                  half(1, 0), half(1, 1),
                  st_p(0), st_p(1), st_f(0), st_f(1),
                  const(w_down)],
        out_specs=[pl.BlockSpec((tm, D_MODEL), lambda i: (tile_f(i), 0)), st_out, st_out],
        out_shape=[jax.ShapeDtypeStruct((geom.rows, D_MODEL), F32),
                   jax.ShapeDtypeStruct((n_tiles, hp, D_FF), F32),
                   jax.ShapeDtypeStruct((n_tiles, hp, D_FF), F32)],
        scratch_shapes=[pltpu.VMEM((tm, D_MODEL), F32)] * 2 + [pltpu.VMEM((tm, D_MODEL), BF16)] * 2
        + [ext(ck[0])] * 4 + [ext(w) for w in ck[1:]] * 2,
        compiler_params=pltpu.CompilerParams(dimension_semantics=("arbitrary",),
                                             vmem_limit_bytes=FFN_PIPELINED_VMEM_LIMIT_BYTES),
    )(x, *mixer_args, gain, w_up, w_up, conv_w, conv_w, conv_b, conv_b, prev, prev, prev, prev, w_down)


def _state_to_halo(state, geom, hp):
    n, w1, ch = state.shape
    if geom.dil == 1:
        rows = state
    else:
        rows = state.transpose(1, 0, 2).reshape(1, w1 * n, ch)
    return jnp.pad(rows, ((0, 0), (hp - rows.shape[1], 0), (0, 0)))


def _halo_to_state(halo, geom, n, w1):
    ch = halo.shape[-1]
    if halo.shape[0] != geom.n_seq:
        halo = halo[geom.tps - 1::geom.tps]
    if geom.dil == 1:
        return halo[:, halo.shape[1] - w1:, :]
    return halo[0, halo.shape[1] - w1 * n:, :].reshape(w1, n, ch).transpose(1, 0, 2)


def _block_diag(w):
    nb, bi, bj = w.shape
    eye = jnp.eye(nb, dtype=w.dtype)
    return (eye[:, None, :, None] * w[:, :, None, :]).reshape(nb * bi, nb * bj)


def _trunk(x_rows, geom, n, pos0, p, paged, rg_h, rg_conv, pool_buf, ffn_buf, ffn_ck):
    depth = p["norm_mix"].shape[0]
    ks, vs, lfs, hs, cs, pbs, fbs = [], [], [], [], [], [], []
    dil = geom.dil
    t_steps = geom.rows // n
    dilp = _round_up(dil, SUBLANES)
    x = x_rows
    for layer in range(depth):
        li = layer // 2
        if layer % 2 == 0:
            w_in = p["ab_w_in"][li]
            a3 = 3 * A_WIDTH
            w = {
                "g": p["norm_mix"][layer][None, :],
                "wqk": w_in[:, :2 * A_WIDTH].astype(BF16),
                "wv": w_in[:, 2 * A_WIDTH:a3].astype(BF16),
                "wf": jnp.pad(w_in[:, a3:a3 + FOX_HEADS], ((0, 0), (0, F_PAD - FOX_HEADS))).astype(BF16),
                "wrg": w_in[:, a3 + FOX_HEADS:].astype(BF16),
                "bf": jnp.pad(p["ab_b_f"][li], (0, F_PAD - FOX_HEADS))[None, :],
                "qg": jnp.tile(p["ab_q_gain"][li], FOX_HEADS)[None, :],
                "kg": jnp.tile(p["ab_k_gain"][li], FOX_HEADS)[None, :],
                "bd": _block_diag(jnp.full((MXU_TILE // HEAD_DIM, HEAD_DIM, HEAD_DIM), 1.0 / HEAD_DIM,
                                           F32)).astype(BF16),
                "cw": p["ab_conv_w"][li], "cb": p["ab_conv_b"][li][None, :],
                "wa": _block_diag(p["ab_w_a"][li]).astype(BF16), "ba": p["ab_b_a"][li][None, :],
                "wx": _block_diag(p["ab_w_x"][li]).astype(BF16), "bx": p["ab_b_x"][li][None, :],
                "lam": p["ab_lambda"][li][None, :],
            }
            hp = geom.halo(RG_CONV)
            cprev = _state_to_halo(rg_conv[li], geom, hp)
            if dil == 1:
                h0 = jnp.pad(rg_h[li][:, None, :], ((0, 0), (0, dilp - 1), (0, 0)))
            else:
                h0 = jnp.pad(rg_h[li][None], ((0, 0), (0, dilp - dil), (0, 0)))
            scale = HEAD_DIM ** -0.5
            q_mult = scale * LOG2E if paged is None else scale
            q, k, v, kt, vt, lft, yrg, cst, hl = _ab_in_call(x, geom, w, cprev, h0, q_mult)

            if paged is None:
                shift = (SCORE_BOUND_SLACK * HEAD_DIM * q_mult * jnp.max(jnp.abs(p["ab_q_gain"][li]))
                         * jnp.max(jnp.abs(p["ab_k_gain"][li])) + 0.5)
                bounded = (shift < MAX_SAFE_SHIFT).astype(jnp.int32).reshape(1)
                cq, ck = _cumsum_call(lft.reshape(n * FOX_HEADS, t_steps), shift.reshape(1, 1))
                cq = cq.reshape(n, FOX_HEADS, GATE_ROWS, t_steps)
                ck = ck.reshape(n, FOX_HEADS, GATE_ROWS, t_steps)
                heads_t = lambda z: z.reshape(n, FOX_HEADS, HEAD_DIM, t_steps)
                attn = _attn_prompt_call(bounded, heads_t(q), cq, heads_t(k), ck, heads_t(v),
                                         min(512, t_steps))
                attn = attn.reshape(n, A_WIDTH, t_steps)
                ks.append(kt.reshape(n, FOX_HEADS, HEAD_DIM, t_steps).transpose(0, 3, 1, 2))
                vs.append(vt.reshape(n, FOX_HEADS, HEAD_DIM, t_steps).transpose(0, 3, 1, 2))
                lfs.append(lft.transpose(0, 2, 1))
                hs.append(hl[:, 0, :])
            else:
                ckt, cvt, clft, page_table = paged

                def heads_bm(z):
                    return z.reshape(t_steps, n, FOX_HEADS, HEAD_DIM).transpose(1, 2, 0, 3)
                lf_new = lft.transpose(2, 1, 0)[:, :, None, :]
                attn = _attn_sample_call(page_table, heads_bm(q), heads_bm(k), heads_bm(v), lf_new,
                                         ckt, cvt, clft, li)
                attn = attn.transpose(2, 0, 1, 3).reshape(geom.rows, A_WIDTH)
                ks.append(kt.reshape(t_steps, FOX_HEADS, HEAD_DIM, n).transpose(3, 0, 1, 2))
                vs.append(vt.reshape(t_steps, FOX_HEADS, HEAD_DIM, n).transpose(3, 0, 1, 2))
                lfs.append(lft.transpose(2, 0, 1))
                hs.append(hl[0, :dil, :])
            cs.append(_halo_to_state(cst, geom, n, RG_CONV - 1))
            w_out = p["ab_w_out"][li].astype(BF16)
            mixer_out = (attn, yrg, w_out[:A_WIDTH], w_out[A_WIDTH:])
        else:
            mixer_out = None
            hp = geom.halo(POOL_BUF + 1)
            prev = _state_to_halo(pool_buf[li], geom, hp)
            x, st = _pool_call(x, geom, p["norm_mix"][layer][None, :], prev,
                               p["pool_w"][li].astype(BF16), p["pool_scale"][li][None, :], pos0)
            pbs.append(_halo_to_state(st, geom, n, POOL_BUF))
        hp = geom.halo(FFN_CONV)
        prev = _state_to_halo(ffn_buf[layer], geom, hp)
        ffn_args = (x, geom, p["norm_ffn"][layer][None, :], p["ffn_w_up"][layer].astype(BF16),
                    p["ffn_conv_w"][layer], p["ffn_conv_b"][layer][None, :], prev,
                    (0.5 * p["ffn_w_down"][layer]).astype(BF16))
        block, ck = ffn_ck
        if block == D_FF and geom.n_tiles > 1 and dil == 1:
            x, sg, sv = _ffn_pipelined_call(*ffn_args, ck, mixer_out=mixer_out)
        else:
            x, sg, sv = _ffn_call(*ffn_args, block, ck, mixer_out=mixer_out)
        fbs.append(_halo_to_state(jnp.concatenate([sg, sv], axis=-1), geom, n, FFN_CONV - 1))
    return x, (jnp.stack(ks), jnp.stack(vs), jnp.stack(lfs), jnp.stack(hs), jnp.stack(cs),
               jnp.stack(pbs), jnp.stack(fbs))


def kernel(x_prompt, x_sample, cache_k, cache_v, cache_logf, state_rg_h, state_rg_conv, state_pool, state_ffn_conv, page_table, norm_mix, norm_ffn, ab_w_in, ab_b_f, ab_q_gain, ab_k_gain, ab_conv_w, ab_conv_b, ab_w_a, ab_b_a, ab_w_x, ab_b_x, ab_lambda, ab_w_out, pool_w, pool_scale, ffn_w_up, ffn_conv_w, ffn_conv_b, ffn_w_down):
    p = {
        "norm_mix": norm_mix, "norm_ffn": norm_ffn,
        "ab_w_in": ab_w_in, "ab_b_f": ab_b_f, "ab_q_gain": ab_q_gain, "ab_k_gain": ab_k_gain,
        "ab_conv_w": ab_conv_w, "ab_conv_b": ab_conv_b, "ab_w_a": ab_w_a, "ab_b_a": ab_b_a,
        "ab_w_x": ab_w_x, "ab_b_x": ab_b_x, "ab_lambda": ab_lambda, "ab_w_out": ab_w_out,
        "pool_w": pool_w, "pool_scale": pool_scale,
        "ffn_w_up": ffn_w_up, "ffn_conv_w": ffn_conv_w, "ffn_conv_b": ffn_conv_b, "ffn_w_down": ffn_w_down,
    }
    depth = norm_mix.shape[0]
    n_ab, n_pool = (depth + 1) // 2, depth // 2

    bsz, t, _ = x_prompt.shape
    geom_p = _Geom(bsz, t, 1, 512)
    y_p, st_p = _trunk(
        x_prompt.reshape(bsz * t, D_MODEL), geom_p, bsz, 0, p, None,
        jnp.zeros((n_ab, bsz, B_WIDTH), F32), jnp.zeros((n_ab, bsz, RG_CONV - 1, B_WIDTH), F32),
        jnp.zeros((n_pool, bsz, POOL_BUF, D_MODEL), F32),
        jnp.zeros((depth, bsz, FFN_CONV - 1, 2 * D_FF), F32), (D_FF, (1024, 1024, 1024)))
    y_prompt = y_p.reshape(bsz, t, D_MODEL)

    db, dt, _ = x_sample.shape
    n_pool_pages, page = cache_k.shape[1], cache_k.shape[2]
    past_len = page_table.shape[1] * page
    geom_s = _Geom(1, dt * db, db, 512)
    paged = (cache_k.transpose(0, 1, 3, 4, 2), cache_v.transpose(0, 1, 3, 4, 2),
             cache_logf.transpose(0, 1, 3, 2), page_table)
    y_s, st_s = _trunk(
        x_sample.transpose(1, 0, 2).reshape(dt * db, D_MODEL), geom_s, db, past_len, p, paged,
        state_rg_h, state_rg_conv, state_pool, state_ffn_conv, (512, (512,)))
    y_sample = y_s.reshape(dt, db, D_MODEL).transpose(1, 0, 2)
    return (y_prompt, y_sample) + st_p + st_s
```
